```python
import math
import jax, jax.numpy as jnp
from jax import lax
import numpy as np

D_MODEL = 1024
BATCH = 16
SEQ = 2048
DEPTH = 2
DEC_BATCH = 8
DEC_SEQ = 8192
PAST_LEN = 128

N_MIXERS = 2
N_LAYERS_A = (DEPTH + 1) // 2
N_LAYERS_B = DEPTH // 2

DIL_CONFIGS = ((128, 1), (512, 4), (2048, 16))
N_DIL = len(DIL_CONFIGS)
HEADS_PER_GROUP = 8
HEAD_DIM = 128
ATTN_WIDTH = HEADS_PER_GROUP * HEAD_DIM
N_BRANCH_HEADS = N_DIL * HEADS_PER_GROUP
QKV_WIDTH = N_DIL * 3 * ATTN_WIDTH

NUM_BUCKETS = 32
MAX_DISTANCE = 1024

POOL_WINDOWS = (2, 4, 8, 16)
N_POOL = len(POOL_WINDOWS)
POOL_WIDTH = D_MODEL
POOL_GC = POOL_WIDTH // N_POOL

N_EXPERTS = 256
TOP_K = 8
N_EXPERT_GROUPS = 8
TOPK_GROUPS = 4
D_EXPERT = 256
ROUTED_SCALE = 2.5
EXPERT_BLOCK = 256

EPS = 1e-6
NEG = -1e30

kernel_name = "hybrid_dilated_pool_moe_encoder"


def rms_norm(x, g):
    xf = x.astype(jnp.float32)
    y = xf * lax.rsqrt(jnp.mean(xf * xf, axis=-1, keepdims=True) + EPS) * g.astype(jnp.float32)
    return y.astype(x.dtype)


def t5_bucket(rel):
    half_b = NUM_BUCKETS // 2
    max_exact = half_b // 2
    n = jnp.abs(rel)
    large = max_exact + (jnp.log(jnp.maximum(n, 1).astype(jnp.float32) / max_exact)
                         / math.log(MAX_DISTANCE / max_exact) * (half_b - max_exact)).astype(jnp.int32)
    large = jnp.minimum(large, half_b - 1)
    return jnp.where(rel > 0, half_b, 0) + jnp.where(n < max_exact, n, large)


def dilated_rel_bias(table_g, dilation, half):
    qi = jnp.arange(half)[:, None]
    ki = jnp.arange(3 * half)[None, :]
    rel = (ki - half - qi) * dilation
    b = table_g[t5_bucket(rel)]
    return jnp.transpose(b, (2, 0, 1)).astype(jnp.float32)


def to_classes(t, dilation):
    B, S = t.shape[:2]
    return jnp.swapaxes(t.reshape((B, S // dilation, dilation) + t.shape[2:]), 1, 2)


def from_classes(t, L):
    B, d = t.shape[:2]
    t = t.reshape((B, d, -1) + t.shape[4:])[:, :, :L]
    t = jnp.swapaxes(t, 1, 2)
    return t.reshape((B, L * d) + t.shape[3:])


def dilated_group_attention(q, k, v, bias, dilation, half):
    B, S, H, E = q.shape
    L = S // dilation
    nb = -(-L // half)
    Lp = nb * half
    qc, kc, vc = to_classes(q, dilation), to_classes(k, dilation), to_classes(v, dilation)
    qb = jnp.pad(qc, ((0, 0), (0, 0), (0, Lp - L), (0, 0), (0, 0))).reshape(B, dilation, nb, half, H, E)
    pad_kv = ((0, 0), (0, 0), (half, Lp - L + half), (0, 0), (0, 0))
    kp, vp = jnp.pad(kc, pad_kv), jnp.pad(vc, pad_kv)

    def band(t):
        return jnp.concatenate(
            [t[:, :, o:o + Lp].reshape(B, dilation, nb, half, H, E) for o in (0, half, 2 * half)], axis=3)

    kb, vb = band(kp), band(vp)
    s = jnp.einsum('brnqhe,brnkhe->brnhqk', qb, kb,
                   preferred_element_type=jnp.float32) * (E ** -0.5) + bias
    qi = jnp.arange(half)[:, None]
    ki = jnp.arange(3 * half)[None, :]
    in_band = jnp.abs(ki - half - qi) <= half
    key_pos = jnp.arange(nb)[:, None] * half + jnp.arange(3 * half)[None, :] - half
    key_ok = (key_pos >= 0) & (key_pos < L)
    mask = in_band[None] & key_ok[:, None, :]
    s = jnp.where(mask[None, None, :, None], s, NEG)
    m = jnp.max(s, axis=-1, keepdims=True)
    p = jnp.exp(s - m)
    den = jnp.sum(p, axis=-1)
    o = jnp.einsum('brnhqk,brnkhe->brnqhe', p, vb, preferred_element_type=jnp.float32)
    o = o / jnp.moveaxis(den, -1, -2)[..., None]
    lse = jnp.moveaxis(m[..., 0] + jnp.log(den), -1, -2)
    return from_classes(o, L), from_classes(lse, L)


def dilated_mixer(h, w_in, w_out, rel_bias_table):
    B, S, _ = h.shape
    qkv = (h @ w_in).reshape(B, S, N_DIL, 3, HEADS_PER_GROUP, HEAD_DIM)
    outs, lses = [], []
    for g, (window, dilation) in enumerate(DIL_CONFIGS):
        half = window // (2 * dilation)
        bias = dilated_rel_bias(rel_bias_table[:, g * HEADS_PER_GROUP:(g + 1) * HEADS_PER_GROUP], dilation, half)
        o, l = dilated_group_attention(qkv[:, :, g, 0], qkv[:, :, g, 1], qkv[:, :, g, 2], bias, dilation, half)
        outs.append(o)
        lses.append(l)
    wts = jax.nn.softmax(jnp.stack(lses, axis=0), axis=0)
    o = jnp.sum(wts[..., None] * jnp.stack(outs, axis=0), axis=0)
    return o.reshape(B, S, ATTN_WIDTH).astype(h.dtype) @ w_out


def pool_mixer(h, w_in, w_grp, ch_scale, w_out):
    B, S, _ = h.shape
    u = (h @ w_in).astype(jnp.float32).reshape(B, S, N_POOL, POOL_GC)
    cs = jnp.pad(jnp.cumsum(u, axis=1), ((0, 0), (1, 0), (0, 0), (0, 0)))
    pos = jnp.arange(S)
    pooled = []
    for g, w in enumerate(POOL_WINDOWS):
        lo = jnp.clip(pos - w // 2, 0, S)
        hi = jnp.clip(pos + w // 2, 0, S)
        cnt = (hi - lo).astype(jnp.float32)
        cg = cs[:, :, g]
        mean = (cg[:, hi] - cg[:, lo]) / cnt[None, :, None]
        pooled.append(mean - u[:, :, g])
    p = jnp.stack(pooled, axis=2)
    z = jnp.einsum('bsgc,gcd->bsgd', p, w_grp.astype(jnp.float32)) * ch_scale.astype(jnp.float32).reshape(N_POOL, POOL_GC)
    return z.reshape(B, S, POOL_WIDTH).astype(h.dtype) @ w_out


def moe_ffn(h, w_router, router_bias, w_gate, w_up, w_down, ws_gate, ws_up, ws_down):
    B, S, D = h.shape
    T = B * S
    x = h.reshape(T, D)
    scores = jax.nn.sigmoid((x @ w_router).astype(jnp.float32))
    biased = scores + router_bias.astype(jnp.float32)
    grp = biased.reshape(T, N_EXPERT_GROUPS, N_EXPERTS // N_EXPERT_GROUPS)
    grp_score = jnp.sum(lax.top_k(grp, 2)[0], axis=-1)
    top_g = lax.top_k(grp_score, TOPK_GROUPS)[1]
    gmask = jnp.any(top_g[..., None] == jnp.arange(N_EXPERT_GROUPS), axis=-2)
    emask = jnp.repeat(gmask, N_EXPERTS // N_EXPERT_GROUPS, axis=1)
    top_e = lax.top_k(jnp.where(emask, biased, NEG), TOP_K)[1]
    gate = jnp.take_along_axis(scores, top_e, axis=1)
    gate = gate / jnp.sum(gate, axis=-1, keepdims=True) * ROUTED_SCALE

    TK = T * TOP_K
    flat_e = top_e.reshape(TK)
    order = jnp.argsort(flat_e)
    se = flat_e[order]
    stok = (order // TOP_K).astype(jnp.int32)
    sg = gate.reshape(TK)[order]
    counts = jnp.bincount(flat_e, length=N_EXPERTS)
    pcounts = (counts + EXPERT_BLOCK - 1) // EXPERT_BLOCK * EXPERT_BLOCK
    pend = jnp.cumsum(pcounts)
    pstart = pend - pcounts
    start = jnp.cumsum(counts) - counts
    dest = pstart[se] + jnp.arange(TK) - start[se]
    P = TK + N_EXPERTS * EXPERT_BLOCK
    nblk = P // EXPERT_BLOCK
    buf_tok = jnp.zeros((P,), jnp.int32).at[dest].set(stok)
    buf_gate = jnp.zeros((P,), jnp.float32).at[dest].set(sg)
    blk_expert = jnp.minimum(jnp.searchsorted(pend, jnp.arange(nblk) * EXPERT_BLOCK, side='right'),
                             N_EXPERTS - 1).astype(jnp.int32)

    y0 = ((jax.nn.silu(x @ ws_gate) * (x @ ws_up)) @ ws_down).astype(jnp.float32)

    def step(y, blk):
        tok, g, e = blk
        xb = x[tok]
        hid = jax.nn.silu(xb @ w_gate[e]) * (xb @ w_up[e])
        yb = (hid @ w_down[e]).astype(jnp.float32) * g[:, None]
        return y.at[tok].add(yb), None

    y, _ = lax.scan(step, y0, (buf_tok.reshape(nblk, EXPERT_BLOCK),
                               buf_gate.reshape(nblk, EXPERT_BLOCK), blk_expert))
    return y.astype(h.dtype).reshape(B, S, D)


def trunk(x, c, w_mod, b_mod, norm1_g, norm2_g, rel_bias_table, w_in_a, w_out_a,
          w_in_b, w_grp_b, pool_scale_b, w_out_b, w_router, router_bias,
          w_gate_e, w_up_e, w_down_e, w_gate_s, w_up_s, w_down_s, final_norm_g):
    for i in range(DEPTH):
        mod = (jax.nn.silu(c) @ w_mod[i] + b_mod[i])[:, None, :]
        sh1, sc1, g1, sh2, sc2, g2 = jnp.split(mod, 6, axis=-1)
        h = rms_norm(x, norm1_g[i]) * (1 + sc1) + sh1
        j = i // N_MIXERS
        if i % N_MIXERS == 0:
            mix = dilated_mixer(h, w_in_a[j], w_out_a[j], rel_bias_table)
        else:
            mix = pool_mixer(h, w_in_b[j], w_grp_b[j], pool_scale_b[j], w_out_b[j])
        x = x + g1 * mix
        h = rms_norm(x, norm2_g[i]) * (1 + sc2) + sh2
        x = x + g2 * moe_ffn(h, w_router[i], router_bias[i], w_gate_e[i], w_up_e[i], w_down_e[i],
                             w_gate_s[i], w_up_s[i], w_down_s[i])
    return rms_norm(x, final_norm_g)


def setup_inputs(seed: int = 0) -> dict:
    key = jax.random.key(seed)
    ks = jax.random.split(key, 26)
    D = D_MODEL

    def nrm(k, shape, scale):
        return jax.random.normal(k, shape, jnp.float32) * scale

    return {
        "x_prompt": nrm(ks[0], (BATCH, SEQ, D), 1.0),
        "x_sample": nrm(ks[1], (DEC_BATCH, DEC_SEQ, D), 1.0),
        "c_prompt": nrm(ks[2], (BATCH, D), 1.0),
        "c_sample": nrm(ks[3], (DEC_BATCH, D), 1.0),
        "w_mod": nrm(ks[4], (DEPTH, D, 6 * D), 0.2 * D ** -0.5),
        "b_mod": nrm(ks[5], (DEPTH, 6 * D), 0.02),
        "norm1_g": 1.0 + nrm(ks[6], (DEPTH, D), 0.02),
        "norm2_g": 1.0 + nrm(ks[7], (DEPTH, D), 0.02),
        "rel_bias_table": nrm(ks[8], (NUM_BUCKETS, N_BRANCH_HEADS), 0.5),
        "w_in_a": nrm(ks[9], (N_LAYERS_A, D, QKV_WIDTH), D ** -0.5),
        "w_out_a": nrm(ks[10], (N_LAYERS_A, ATTN_WIDTH, D), ATTN_WIDTH ** -0.5),
        "w_in_b": nrm(ks[11], (N_LAYERS_B, D, POOL_WIDTH), D ** -0.5),
        "w_grp_b": nrm(ks[12], (N_LAYERS_B, N_POOL, POOL_GC, POOL_GC), POOL_GC ** -0.5),
        "pool_scale_b": 1.0 + nrm(ks[13], (N_LAYERS_B, POOL_WIDTH), 0.1),
        "w_out_b": nrm(ks[14], (N_LAYERS_B, POOL_WIDTH, D), POOL_WIDTH ** -0.5),
        "w_router": nrm(ks[15], (DEPTH, D, N_EXPERTS), D ** -0.5),
        "router_bias": nrm(ks[16], (DEPTH, N_EXPERTS), 0.01),
        "w_gate_e": nrm(ks[17], (DEPTH, N_EXPERTS, D, D_EXPERT), D ** -0.5),
        "w_up_e": nrm(ks[18], (DEPTH, N_EXPERTS, D, D_EXPERT), D ** -0.5),
        "w_down_e": nrm(ks[19], (DEPTH, N_EXPERTS, D_EXPERT, D), D_EXPERT ** -0.5),
        "w_gate_s": nrm(ks[20], (DEPTH, D, D_EXPERT), D ** -0.5),
        "w_up_s": nrm(ks[21], (DEPTH, D, D_EXPERT), D ** -0.5),
        "w_down_s": nrm(ks[22], (DEPTH, D_EXPERT, D), D_EXPERT ** -0.5),
        "final_norm_g": 1.0 + nrm(ks[23], (D,), 0.02),
    }


def reference(x_prompt, x_sample, c_prompt, c_sample, w_mod, b_mod, norm1_g, norm2_g, rel_bias_table,
              w_in_a, w_out_a, w_in_b, w_grp_b, pool_scale_b, w_out_b, w_router, router_bias,
              w_gate_e, w_up_e, w_down_e, w_gate_s, w_up_s, w_down_s, final_norm_g):
    y_prompt = trunk(x_prompt, c_prompt, w_mod, b_mod, norm1_g, norm2_g, rel_bias_table, w_in_a, w_out_a,
                     w_in_b, w_grp_b, pool_scale_b, w_out_b, w_router, router_bias,
                     w_gate_e, w_up_e, w_down_e, w_gate_s, w_up_s, w_down_s, final_norm_g)
    y_sample = trunk(x_sample, c_sample, w_mod, b_mod, norm1_g, norm2_g, rel_bias_table, w_in_a, w_out_a,
                     w_in_b, w_grp_b, pool_scale_b, w_out_b, w_router, router_bias,
                     w_gate_e, w_up_e, w_down_e, w_gate_s, w_up_s, w_down_s, final_norm_g)
    return (y_prompt, y_sample)
```

```python
import functools
import math

import jax
import jax.numpy as jnp
from jax import lax
from jax.experimental import pallas as pl
from jax.experimental.pallas import tpu as pltpu

F32 = jnp.float32
BF16 = jnp.bfloat16
I32 = jnp.int32

EPS = 1e-6
NEG = -1e30

DIL_CONFIGS = ((128, 1), (512, 4), (2048, 16))
HEADS_PER_GROUP = 8
HEAD_DIM = 128
NUM_BUCKETS = 32
MAX_DISTANCE = 1024
POOL_WINDOWS = (2, 4, 8, 16)
TOP_K = 8
N_EXPERT_GROUPS = 8
TOPK_GROUPS = 4
ROUTED_SCALE = 2.5

ROW_TILE = 512
EXPERT_BLOCK = 256
MOE_TOKENS = 256
ATTN_SUPER = 1024
ATTN_HALF = 64
POOL_HALO = 8
VMEM_LIMIT = 56 * 1024 * 1024

NT_DIMS = (((1,), (1,)), ((), ()))


def _dot(a, b):
    return jnp.dot(a, b, preferred_element_type=F32)


def _split_bf16(a):
    hi = a.astype(BF16)
    lo = (a - hi.astype(F32)).astype(BF16)
    return hi, lo


def _norm_mod(x, g, sc, sh):
    ms = jnp.mean(x * x, axis=-1, keepdims=True)
    y = x * lax.rsqrt(ms + EPS) * g
    return y * (1.0 + sc) + sh


def _silu(x):
    return x * jax.nn.sigmoid(x)


class _Geom:
    def __init__(self, bp, sp, bs, ss):
        self.bp, self.sp, self.bs, self.ss = bp, sp, bs, ss
        self.tp, self.ts = bp * sp, bs * ss
        self.t = self.tp + self.ts
        self.nb = bp + bs

    def seq_of_row(self, row):
        return jnp.where(row < self.tp, row // self.sp, self.bp + (row - self.tp) // self.ss)

    def pos_and_len(self, row):
        in_p = row < self.tp
        pos = jnp.where(in_p, row % self.sp, (row - self.tp) % self.ss)
        return pos, jnp.where(in_p, self.sp, self.ss)


def _mod_kernel(c_ref, w_ref, b_ref, o_ref):
    a_hi, a_lo = _split_bf16(_silu(c_ref[...]))
    w_hi, w_lo = _split_bf16(w_ref[...])
    o_ref[...] = _dot(a_hi, w_hi) + _dot(a_hi, w_lo) + _dot(a_lo, w_hi) + b_ref[...]


def _modulation(c, w_mod, b_mod):
    depth, d, n = w_mod.shape
    nb = c.shape[0]
    tn = 1024
    return pl.pallas_call(
        _mod_kernel,
        out_shape=jax.ShapeDtypeStruct((depth, nb, n), F32),
        grid=(depth, n // tn),
        in_specs=[
            pl.BlockSpec((nb, d), lambda l, j: (0, 0)),
            pl.BlockSpec((None, d, tn), lambda l, j: (l, 0, j)),
            pl.BlockSpec((None, 1, tn), lambda l, j: (l, 0, j)),
        ],
        out_specs=pl.BlockSpec((None, nb, tn), lambda l, j: (l, 0, j)),
        compiler_params=pltpu.CompilerParams(vmem_limit_bytes=VMEM_LIMIT),
        name="modulation",
    )(c, w_mod, b_mod.reshape(depth, 1, n))


def _proj_kernel(x_ref, mod_ref, g_ref, w_ref, o_ref, h_ref):
    @pl.when(pl.program_id(1) == 0)
    def _():
        h = _norm_mod(x_ref[...], g_ref[...], mod_ref[1:2, :], mod_ref[0:1, :])
        h_ref[...] = h.astype(BF16)

    o_ref[...] = _dot(h_ref[...], w_ref[...])


def _norm_proj(geom, x, mod4, layer, norm_g, w_bf16, tn):
    t, d = x.shape
    n = w_bf16.shape[1]
    tm = ROW_TILE
    return pl.pallas_call(
        _proj_kernel,
        out_shape=jax.ShapeDtypeStruct((t, n), F32),
        grid=(t // tm, n // tn),
        in_specs=[
            pl.BlockSpec((tm, d), lambda i, j: (i, 0)),
            pl.BlockSpec((None, None, 6, d), lambda i, j: (layer, geom.seq_of_row(i * tm), 0, 0)),
            pl.BlockSpec((1, d), lambda i, j: (0, 0)),
            pl.BlockSpec((d, tn), lambda i, j: (0, j)),
        ],
        out_specs=pl.BlockSpec((tm, tn), lambda i, j: (i, j)),
        scratch_shapes=[pltpu.VMEM((tm, d), BF16)],
        compiler_params=pltpu.CompilerParams(
            dimension_semantics=("arbitrary", "arbitrary"), vmem_limit_bytes=VMEM_LIMIT),
        name="norm_proj",
    )(x, mod4, norm_g.reshape(1, d), w_bf16)


def _t5_bucket(rel):
    half_b = NUM_BUCKETS // 2
    max_exact = half_b // 2
    n = jnp.abs(rel)
    large = max_exact + (jnp.log(jnp.maximum(n, 1).astype(F32) / max_exact)
                         / math.log(MAX_DISTANCE / max_exact) * (half_b - max_exact)).astype(I32)
    large = jnp.minimum(large, half_b - 1)
    return jnp.where(rel > 0, half_b, 0) + jnp.where(n < max_exact, n, large)


def _attn_bias_tables(rel_bias_table):
    half = ATTN_HALF
    qi = jnp.arange(half)[:, None]
    ki = jnp.arange(3 * half)[None, :]
    off = ki - half - qi
    in_band = jnp.abs(off) <= half
    per_branch = []
    for g, (window, dilation) in enumerate(DIL_CONFIGS):
        assert window // (2 * dilation) == half
        tab = rel_bias_table[:, g * HEADS_PER_GROUP:(g + 1) * HEADS_PER_GROUP]
        b = jnp.transpose(tab[_t5_bucket(off * dilation)], (2, 0, 1)).astype(F32)
        variants = []
        for var in range(4):
            ok = in_band
            if var & 1:
                ok = ok & (ki >= half)
            if var & 2:
                ok = ok & (ki < 2 * half)
            variants.append(jnp.where(ok[None], b, NEG))
        per_branch.append(jnp.stack(variants, axis=1))
    return jnp.stack(per_branch, axis=1)


def _attn_branch(g, d, q_ref, k_ref, v_ref, bias_ref, o_ref, mx_ref, wt_ref, os_ref, ls_ref, seq_len):
    half = ATTN_HALF
    scale = HEAD_DIM ** -0.5
    units = ATTN_SUPER // half
    ch = half * d
    last = g == len(DIL_CONFIGS) - 1

    def rows(ref, start):
        return ref[pl.ds(start, half, stride=d), :]

    def super_chunk(sc, carry):
        base_sc = pl.multiple_of(sc * ATTN_SUPER, ATTN_SUPER)

        def unit(u, c2):
            c = u // d
            r = u % d
            off = c * ch
            base = base_sc + off
            no_prev = (base == 0).astype(I32)
            no_next = (base + ch >= seq_len).astype(I32)
            var = no_prev + 2 * no_next
            s_prev = jnp.maximum(base - ch, 0) + r
            s_next = jnp.minimum(base + ch, seq_len - ch) + r
            s_cur = base + r
            q = rows(q_ref, s_cur).astype(BF16)
            kk = jnp.concatenate([rows(k_ref, s_prev), rows(k_ref, s_cur), rows(k_ref, s_next)],
                                 axis=0).astype(BF16)
            vv = jnp.concatenate([rows(v_ref, s_prev), rows(v_ref, s_cur), rows(v_ref, s_next)],
                                 axis=0).astype(BF16)
            s = lax.dot_general(q, kk, NT_DIMS, preferred_element_type=F32) * scale + bias_ref[g, var]
            m = jnp.max(s, axis=-1, keepdims=True)
            p = jnp.exp(s - m)
            den = jnp.sum(p, axis=-1, keepdims=True)
            o = _dot(p.astype(BF16), vv) / den
            lse = m + jnp.log(den)
            os_ref[pl.ds(off + r, half, stride=d), :] = o
            ls_ref[pl.ds(off + r, half, stride=d), :] = jnp.broadcast_to(lse, (half, HEAD_DIM))
            return c2

        lax.fori_loop(0, units, unit, 0)

        span = pl.ds(base_sc, ATTN_SUPER)
        og, lg = os_ref[...], ls_ref[...]
        if g == 0:
            o_ref[span, :] = og
            mx_ref[span, :] = lg
            wt_ref[span, :] = jnp.ones_like(lg)
        else:
            m_old = mx_ref[span, :]
            m_new = jnp.maximum(m_old, lg)
            a = jnp.exp(m_old - m_new)
            b = jnp.exp(lg - m_new)
            acc = o_ref[span, :] * a + og * b
            wt = wt_ref[span, :] * a + b
            if last:
                o_ref[span, :] = acc / wt
            else:
                o_ref[span, :] = acc
                mx_ref[span, :] = m_new
                wt_ref[span, :] = wt
        return carry

    lax.fori_loop(0, seq_len // ATTN_SUPER, super_chunk, 0)


def _attn_kernel(q_ref, k_ref, v_ref, bias_ref, *rest, seq_len):
    o_ref, mx_ref, wt_ref, os_ref, ls_ref = rest[-5:]
    for g, (_, d) in enumerate(DIL_CONFIGS):
        @pl.when(pl.program_id(2) == g)
        def _(g=g, d=d):
            _attn_branch(g, d, q_ref, k_ref, v_ref, bias_ref, o_ref, mx_ref, wt_ref, os_ref, ls_ref, seq_len)


def _attention_group(qkv, bias, prev_out, *, n_seq, seq_len, row_block0, t_total):
    assert seq_len % ATTN_SUPER == 0
    for _, d in DIL_CONFIGS:
        assert ATTN_SUPER % (ATTN_HALF * d) == 0
    h = HEADS_PER_GROUP
    width = h * HEAD_DIM
    n_br = len(DIL_CONFIGS)

    def col_spec(c):
        return pl.BlockSpec((seq_len, HEAD_DIM), lambda b, hh, g: (row_block0 + b, (g * 3 + c) * h + hh))

    in_specs = [col_spec(c) for c in range(3)]
    in_specs.append(pl.BlockSpec((None, n_br, 4, ATTN_HALF, 3 * ATTN_HALF),
                                 lambda b, hh, g: (hh, 0, 0, 0, 0)))
    args = [qkv] * 3 + [bias]
    aliases = {}
    if prev_out is not None:
        in_specs.append(pl.BlockSpec(memory_space=pl.ANY))
        args.append(prev_out)
        aliases = {len(args) - 1: 0}
    return pl.pallas_call(
        functools.partial(_attn_kernel, seq_len=seq_len),
        out_shape=jax.ShapeDtypeStruct((t_total, width), F32),
        grid=(n_seq, h, n_br),
        in_specs=in_specs,
        out_specs=pl.BlockSpec((seq_len, HEAD_DIM), lambda b, hh, g: (row_block0 + b, hh)),
        scratch_shapes=[pltpu.VMEM((seq_len, HEAD_DIM), F32),
                        pltpu.VMEM((seq_len, HEAD_DIM), F32),
                        pltpu.VMEM((ATTN_SUPER, HEAD_DIM), F32),
                        pltpu.VMEM((ATTN_SUPER, HEAD_DIM), F32)],
        input_output_aliases=aliases,
        compiler_params=pltpu.CompilerParams(
            dimension_semantics=("arbitrary", "arbitrary", "arbitrary"), vmem_limit_bytes=VMEM_LIMIT),
        name="dilated_attention",
    )(*args)


def _route(h2, wrh_ref, wrl_ref, rb_ref, utri_ref, run_ref, te_ref, gt_ref, rk_ref, cnt_ref):
    n_exp = wrh_ref.shape[0]
    tm = h2.shape[0]
    per_group = n_exp // N_EXPERT_GROUPS
    h_hi, h_lo = _split_bf16(h2)
    wrh = wrh_ref[...]
    logits = (lax.dot_general(wrh, h_hi, NT_DIMS, preferred_element_type=F32)
              + lax.dot_general(wrh, h_lo, NT_DIMS, preferred_element_type=F32)
              + lax.dot_general(wrl_ref[...], h_hi, NT_DIMS, preferred_element_type=F32))
    scores = jax.nn.sigmoid(logits)
    biased = scores + rb_ref[...]

    gi = lax.broadcasted_iota(I32, (per_group, tm), 0).astype(F32)
    group_vals, group_scores = [], []
    for g in range(N_EXPERT_GROUPS):
        v = biased[g * per_group:(g + 1) * per_group, :]
        m1 = jnp.max(v, axis=0, keepdims=True)
        i1 = jnp.min(jnp.where(v == m1, gi, float(per_group)), axis=0, keepdims=True)
        m2 = jnp.max(jnp.where(gi == i1, -jnp.inf, v), axis=0, keepdims=True)
        group_vals.append(v)
        group_scores.append(m1 + m2)
    gs = jnp.concatenate(group_scores, axis=0)
    g_iota = lax.broadcasted_iota(I32, gs.shape, 0)
    rank = jnp.zeros(gs.shape, I32)
    for gp in range(N_EXPERT_GROUPS):
        row = gs[gp:gp + 1, :]
        beats = jnp.where(row > gs, 1, jnp.where(row == gs, jnp.where(g_iota > gp, 1, 0), 0))
        rank = rank + beats
    keep = rank < TOPK_GROUPS
    cur = jnp.concatenate(
        [jnp.where(keep[g:g + 1, :], group_vals[g], NEG) for g in range(N_EXPERT_GROUPS)], axis=0)

    e_iota = lax.broadcasted_iota(I32, (n_exp, tm), 0).astype(F32)
    picks, gates = [], []
    for _ in range(TOP_K):
        mk = jnp.max(cur, axis=0, keepdims=True)
        ik = jnp.min(jnp.where(cur == mk, e_iota, float(n_exp)), axis=0, keepdims=True)
        sel = e_iota == ik
        gates.append(jnp.sum(jnp.where(sel, scores, 0.0), axis=0, keepdims=True))
        cur = jnp.where(sel, -jnp.inf, cur)
        picks.append(ik)
    gt = jnp.concatenate(gates, axis=0)
    gt = gt / jnp.sum(gt, axis=0, keepdims=True) * ROUTED_SCALE
    te_ref[...] = jnp.concatenate(picks, axis=0).astype(I32)
    gt_ref[...] = gt

    hot = jnp.zeros((n_exp, tm), F32)
    for k in range(TOP_K):
        hot = hot + jnp.where(e_iota == picks[k], 1.0, 0.0)
    before = _dot(hot.astype(BF16), utri_ref[...]) + run_ref[...]
    ranks = [jnp.sum(jnp.where(e_iota == picks[k], before, 0.0), axis=0, keepdims=True)
             for k in range(TOP_K)]
    rk_ref[...] = jnp.concatenate(ranks, axis=0).astype(I32)
    run_ref[...] = run_ref[...] + jnp.sum(hot, axis=1, keepdims=True)
    cnt_ref[...] = run_ref[...]


def _post_common(mix, x_ref, mod_ref, n2_ref, wrh_ref, wrl_ref, rb_ref, utri_ref,
                 x1_ref, h2_ref, te_ref, gt_ref, rk_ref, cnt_ref, run_ref):
    @pl.when(pl.program_id(0) == 0)
    def _():
        run_ref[...] = jnp.zeros(run_ref.shape, F32)

    x1 = x_ref[...] + mod_ref[2:3, :] * mix
    x1_ref[...] = x1
    h2 = _norm_mod(x1, n2_ref[...], mod_ref[4:5, :], mod_ref[3:4, :])
    h2_ref[...] = h2
    _route(h2, wrh_ref, wrl_ref, rb_ref, utri_ref, run_ref, te_ref, gt_ref, rk_ref, cnt_ref)


def _post_attn_kernel(a_ref, wo_ref, *rest):
    mix = _dot(a_ref[...].astype(BF16), wo_ref[...])
    _post_common(mix, *rest)


def _post_pool_kernel(u_ref, up_ref, un_ref, wg_ref, cs_ref, wo_ref, *rest, geom):
    ext_ref = rest[-1]
    rest = rest[:-1]
    tm, d = u_ref.shape
    halo = POOL_HALO
    row0 = pl.program_id(0) * tm
    pos0, slen = geom.pos_and_len(row0)
    u = u_ref[...]
    ext_ref[pl.ds(halo, tm), :] = u
    ext_ref[pl.ds(0, halo), :] = jnp.where(pos0 > 0, up_ref[...], 0.0)
    ext_ref[pl.ds(halo + tm, halo), :] = jnp.where(pos0 + tm < slen, un_ref[...], 0.0)
    pos = pos0 + lax.broadcasted_iota(I32, (tm, 1), 0)
    gc = d // len(POOL_WINDOWS)
    zs = []
    for g, w in enumerate(POOL_WINDOWS):
        hw = w // 2
        assert hw <= halo
        cols = slice(g * gc, (g + 1) * gc)
        acc = ext_ref[pl.ds(halo - hw, tm), cols]
        for o in range(-hw + 1, hw):
            acc = acc + ext_ref[pl.ds(halo + o, tm), cols]
        cnt = (jnp.minimum(pos + hw, slen) - jnp.maximum(pos - hw, 0)).astype(F32)
        p = acc / cnt - u[:, cols]
        zs.append(_dot(p.astype(BF16), wg_ref[g]) * cs_ref[:, cols])
    z = jnp.concatenate(zs, axis=1)
    mix = _dot(z.astype(BF16), wo_ref[...])
    _post_common(mix, *rest)


def _post_mixer(geom, layer, mixer_args, mixer_specs, body, x, mod4, norm2_g, w_router, router_bias,
                extra_scratch=()):
    t, d = x.shape
    n_exp = w_router.shape[1]
    tm = ROW_TILE
    wr_t = w_router.T
    wr_hi = wr_t.astype(BF16)
    wr_lo = (wr_t - wr_hi.astype(F32)).astype(BF16)
    utri = (jnp.arange(tm)[:, None] < jnp.arange(tm)[None, :]).astype(BF16)
    row = lambda i: (i, 0)
    col = lambda i: (0, i)
    const = lambda i: (0, 0)
    in_specs = list(mixer_specs) + [
        pl.BlockSpec((tm, d), row),
        pl.BlockSpec((None, None, 6, d), lambda i: (layer, geom.seq_of_row(i * tm), 0, 0)),
        pl.BlockSpec((1, d), const),
        pl.BlockSpec((n_exp, d), const),
        pl.BlockSpec((n_exp, d), const),
        pl.BlockSpec((n_exp, 1), const),
        pl.BlockSpec((tm, tm), const),
    ]
    out_shape = [
        jax.ShapeDtypeStruct((t, d), F32),
        jax.ShapeDtypeStruct((t, d), F32),
        jax.ShapeDtypeStruct((TOP_K, t), I32),
        jax.ShapeDtypeStruct((TOP_K, t), F32),
        jax.ShapeDtypeStruct((TOP_K, t), I32),
        jax.ShapeDtypeStruct((n_exp, 1), F32),
    ]
    out_specs = [
        pl.BlockSpec((tm, d), row),
        pl.BlockSpec((tm, d), row),
        pl.BlockSpec((TOP_K, tm), col),
        pl.BlockSpec((TOP_K, tm), col),
        pl.BlockSpec((TOP_K, tm), col),
        pl.BlockSpec((n_exp, 1), const),
    ]
    return pl.pallas_call(
        body,
        out_shape=out_shape,
        grid=(t // tm,),
        in_specs=in_specs,
        out_specs=out_specs,
        scratch_shapes=[pltpu.VMEM((n_exp, 1), F32)] + list(extra_scratch),
        compiler_params=pltpu.CompilerParams(
            dimension_semantics=("arbitrary",), vmem_limit_bytes=VMEM_LIMIT),
        name="post_mixer",
    )(*mixer_args, x, mod4, norm2_g.reshape(1, d), wr_hi, wr_lo, router_bias.reshape(n_exp, 1), utri)


def _dest_kernel(te_ref, rk_ref, ps_ref, d_ref):
    n_exp = ps_ref.shape[0]
    te = te_ref[...]
    e_iota = lax.broadcasted_iota(I32, (n_exp, te.shape[1]), 0)
    ps = ps_ref[...]
    rows = [jnp.sum(jnp.where(e_iota == te[k:k + 1, :], ps, 0.0), axis=0, keepdims=True)
            for k in range(TOP_K)]
    d_ref[...] = jnp.concatenate(rows, axis=0).astype(I32) + rk_ref[...]


def _dest_slots(te, rk, pstart):
    k, t = te.shape
    n_exp = pstart.shape[0]
    tm = ROW_TILE
    col = lambda i: (0, i)
    return pl.pallas_call(
        _dest_kernel,
        out_shape=jax.ShapeDtypeStruct((k, t), I32),
        grid=(t // tm,),
        in_specs=[pl.BlockSpec((k, tm), col), pl.BlockSpec((k, tm), col),
                  pl.BlockSpec((n_exp, 1), lambda i: (0, 0))],
        out_specs=pl.BlockSpec((k, tm), col),
        name="moe_dest",
    )(te, rk, pstart.astype(F32).reshape(n_exp, 1))


def _zero_kernel(lb_ref, o_ref):
    o_ref[...] = jnp.zeros(o_ref.shape, F32)


def _zero_tail_blocks(last_blk, p_rows, d):
    return pl.pallas_call(
        _zero_kernel,
        out_shape=jax.ShapeDtypeStruct((p_rows, d), F32),
        grid_spec=pltpu.PrefetchScalarGridSpec(
            num_scalar_prefetch=1,
            grid=(last_blk.shape[0],),
            in_specs=[],
            out_specs=pl.BlockSpec((EXPERT_BLOCK, d), lambda e, lb: (lb[e], 0)),
        ),
        name="moe_zero_tails",
    )(last_blk)


def _dispatch_kernel(dest_ref, h_ref, xs_in_ref, xs_ref, sem):
    del xs_in_ref
    tt = h_ref.shape[0]

    def issue(t, carry):
        for k in range(TOP_K):
            pltpu.make_async_copy(h_ref.at[pl.ds(t, 1), :],
                                  xs_ref.at[pl.ds(dest_ref[k, t], 1), :], sem).start()
        return carry

    lax.fori_loop(0, tt, issue, 0)
    for k in range(TOP_K):
        pltpu.make_async_copy(h_ref, xs_ref.at[pl.ds(0, tt), :], sem).wait()


def _dispatch(dest, h2, xs_zeroed):
    t, d = h2.shape
    tt = MOE_TOKENS
    return pl.pallas_call(
        _dispatch_kernel,
        out_shape=jax.ShapeDtypeStruct(xs_zeroed.shape, F32),
        grid=(t // tt,),
        in_specs=[
            pl.BlockSpec((TOP_K, tt), lambda i: (0, i), memory_space=pltpu.SMEM),
            pl.BlockSpec((tt, d), lambda i: (i, 0)),
            pl.BlockSpec(memory_space=pl.ANY),
        ],
        out_specs=pl.BlockSpec(memory_space=pl.ANY),
        scratch_shapes=[pltpu.SemaphoreType.DMA(())],
        input_output_aliases={2: 0},
        compiler_params=pltpu.CompilerParams(
            dimension_semantics=("arbitrary",), has_side_effects=True),
        name="moe_dispatch",
    )(dest, h2, xs_zeroed)


def _expert_kernel(be_ref, nu_ref, x_ref, wg_ref, wu_ref, wd_ref, o_ref):
    @pl.when(pl.program_id(0) < nu_ref[0])
    def _():
        x = x_ref[...].astype(BF16)
        hid = _silu(_dot(x, wg_ref[...])) * _dot(x, wu_ref[...])
        o_ref[...] = _dot(hid.astype(BF16), wd_ref[...])


def _experts(blk_expert, n_used, xs, wg, wu, wd):
    p_rows, d = xs.shape
    de = wg.shape[2]
    blk = lambda i, be, nu: (jnp.minimum(i, nu[0] - 1), 0)
    return pl.pallas_call(
        _expert_kernel,
        out_shape=jax.ShapeDtypeStruct((p_rows, d), F32),
        grid_spec=pltpu.PrefetchScalarGridSpec(
            num_scalar_prefetch=2,
            grid=(p_rows // EXPERT_BLOCK,),
            in_specs=[
                pl.BlockSpec((EXPERT_BLOCK, d), blk),
                pl.BlockSpec((None, d, de), lambda i, be, nu: (be[i], 0, 0)),
                pl.BlockSpec((None, d, de), lambda i, be, nu: (be[i], 0, 0)),
                pl.BlockSpec((None, de, d), lambda i, be, nu: (be[i], 0, 0)),
            ],
            out_specs=pl.BlockSpec((EXPERT_BLOCK, d), blk),
        ),
        compiler_params=pltpu.CompilerParams(
            dimension_semantics=("arbitrary",), vmem_limit_bytes=VMEM_LIMIT),
        name="moe_experts",
    )(blk_expert, n_used, xs, wg, wu, wd)


def _combine_kernel(dest_ref, x1_ref, h_ref, gt_ref, mod_ref, wsg_ref, wsu_ref, wsd_ref, fg_ref, ys_ref,
                    o_ref, buf_ref, sem, *, final_norm):
    tt = x1_ref.shape[0]

    def issue(t, carry):
        for k in range(TOP_K):
            pltpu.make_async_copy(ys_ref.at[pl.ds(dest_ref[k, t], 1), :],
                                  buf_ref.at[k, pl.ds(t, 1), :], sem).start()
        return carry

    lax.fori_loop(0, tt, issue, 0)
    hs = h_ref[...].astype(BF16)
    hid = _silu(_dot(hs, wsg_ref[...])) * _dot(hs, wsu_ref[...])
    y = _dot(hid.astype(BF16), wsd_ref[...])
    for k in range(TOP_K):
        pltpu.make_async_copy(ys_ref.at[pl.ds(0, tt), :], buf_ref.at[k], sem).wait()
    gt = gt_ref[...]
    for k in range(TOP_K):
        y = y + buf_ref[k] * gt[:, k:k + 1]
    x2 = x1_ref[...] + mod_ref[5:6, :] * y
    if final_norm:
        ms = jnp.mean(x2 * x2, axis=-1, keepdims=True)
        x2 = x2 * lax.rsqrt(ms + EPS) * fg_ref[...]
    o_ref[...] = x2


def _combine(geom, layer, dest, x1, h2, gate_t, mod4, wsg, wsu, wsd, final_g, ys, final_norm):
    t, d = x1.shape
    de = wsg.shape[1]
    tt = MOE_TOKENS
    row = lambda i: (i, 0)
    const = lambda i: (0, 0)
    return pl.pallas_call(
        functools.partial(_combine_kernel, final_norm=final_norm),
        out_shape=jax.ShapeDtypeStruct((t, d), F32),
        grid=(t // tt,),
        in_specs=[
            pl.BlockSpec((TOP_K, tt), lambda i: (0, i), memory_space=pltpu.SMEM),
            pl.BlockSpec((tt, d), row),
            pl.BlockSpec((tt, d), row),
            pl.BlockSpec((tt, TOP_K), row),
            pl.BlockSpec((None, None, 6, d), lambda i: (layer, geom.seq_of_row(i * tt), 0, 0)),
            pl.BlockSpec((d, de), const),
            pl.BlockSpec((d, de), const),
            pl.BlockSpec((de, d), const),
            pl.BlockSpec((1, d), const),
            pl.BlockSpec(memory_space=pl.ANY),
        ],
        out_specs=pl.BlockSpec((tt, d), row),
        scratch_shapes=[pltpu.VMEM((TOP_K, tt, d), F32), pltpu.SemaphoreType.DMA(())],
        compiler_params=pltpu.CompilerParams(
            dimension_semantics=("arbitrary",), vmem_limit_bytes=VMEM_LIMIT),
        name="moe_combine",
    )(dest, x1, h2, gate_t, mod4, wsg, wsu, wsd, final_g.reshape(1, d), ys)


def _moe(geom, layer, x1, h2, te, gt, rk, cnt, mod4, wg, wu, wd, wsg, wsu, wsd, final_g, final_norm):
    t, d = x1.shape
    n_exp = wg.shape[0]
    p_rows = t * TOP_K + n_exp * EXPERT_BLOCK
    n_blk = p_rows // EXPERT_BLOCK
    counts = cnt[:, 0].astype(I32)
    pcounts = (counts + EXPERT_BLOCK - 1) // EXPERT_BLOCK * EXPERT_BLOCK
    pend = jnp.cumsum(pcounts)
    pstart = pend - pcounts
    n_used = pend[-1] // EXPERT_BLOCK
    blk_ids = jnp.arange(n_blk, dtype=I32)
    be = jnp.minimum(jnp.searchsorted(pend, blk_ids * EXPERT_BLOCK, side='right'), n_exp - 1).astype(I32)
    be = jnp.where(blk_ids < n_used, be, be[n_used - 1])
    last_blk = jnp.maximum(pend // EXPERT_BLOCK - 1, 0).astype(I32)

    dest = _dest_slots(te, rk, pstart)
    xs = _dispatch(dest, h2, _zero_tail_blocks(last_blk, p_rows, d))
    ys = _experts(be, n_used.reshape(1).astype(I32), xs, wg, wu, wd)
    return _combine(geom, layer, dest, x1, h2, gt.T, mod4, wsg, wsu, wsd, final_g, ys, final_norm)


def kernel(x_prompt, x_sample, c_prompt, c_sample, w_mod, b_mod, norm1_g, norm2_g, rel_bias_table, w_in_a, w_out_a, w_in_b, w_grp_b, pool_scale_b, w_out_b, w_router, router_bias, w_gate_e, w_up_e, w_down_e, w_gate_s, w_up_s, w_down_s, final_norm_g):
    bp, sp, d = x_prompt.shape
    bs, ss, _ = x_sample.shape
    geom = _Geom(bp, sp, bs, ss)
    depth = w_mod.shape[0]
    tm = ROW_TILE
    assert sp % ATTN_SUPER == 0 and ss % ATTN_SUPER == 0 and geom.tp % ss == 0
    assert sp % tm == 0 and ss % tm == 0 and tm % MOE_TOKENS == 0

    x = jnp.concatenate([x_prompt.reshape(geom.tp, d), x_sample.reshape(geom.ts, d)], axis=0)
    c = jnp.concatenate([c_prompt, c_sample], axis=0)
    mod4 = _modulation(c, w_mod, b_mod).reshape(depth, geom.nb, 6, d)
    bias = _attn_bias_tables(rel_bias_table)

    for i in range(depth):
        j = i // 2
        if i % 2 == 0:
            qkv = _norm_proj(geom, x, mod4, i, norm1_g[i], w_in_a[j].astype(BF16), 1536)
            attn = _attention_group(qkv, bias, None, n_seq=bp, seq_len=sp, row_block0=0, t_total=geom.t)
            attn = _attention_group(qkv, bias, attn, n_seq=bs, seq_len=ss, row_block0=geom.tp // ss,
                                    t_total=geom.t)
            mixer_args = (attn, w_out_a[j].astype(BF16))
            mixer_specs = (pl.BlockSpec((tm, d), lambda r: (r, 0)),
                           pl.BlockSpec((d, d), lambda r: (0, 0)))
            body, extra = _post_attn_kernel, ()
        else:
            u = _norm_proj(geom, x, mod4, i, norm1_g[i], w_in_b[j].astype(BF16), 1024)
            hb = tm // POOL_HALO
            n_hb = geom.t // POOL_HALO
            mixer_args = (u, u, u, w_grp_b[j].astype(BF16), pool_scale_b[j].reshape(1, d),
                          w_out_b[j].astype(BF16))
            gc = d // len(POOL_WINDOWS)
            mixer_specs = (pl.BlockSpec((tm, d), lambda r: (r, 0)),
                           pl.BlockSpec((POOL_HALO, d), lambda r: (jnp.maximum(r * hb - 1, 0), 0)),
                           pl.BlockSpec((POOL_HALO, d), lambda r: (jnp.minimum((r + 1) * hb, n_hb - 1), 0)),
                           pl.BlockSpec((len(POOL_WINDOWS), gc, gc), lambda r: (0, 0, 0)),
                           pl.BlockSpec((1, d), lambda r: (0, 0)),
                           pl.BlockSpec((d, d), lambda r: (0, 0)))
            body = functools.partial(_post_pool_kernel, geom=geom)
            extra = (pltpu.VMEM((tm + 2 * POOL_HALO, d), F32),)
        x1, h2, te, gt, rk, cnt = _post_mixer(geom, i, mixer_args, mixer_specs, body, x, mod4, norm2_g[i],
                                              w_router[i], router_bias[i], extra)
        x = _moe(geom, i, x1, h2, te, gt, rk, cnt, mod4,
                 w_gate_e[i].astype(BF16), w_up_e[i].astype(BF16), w_down_e[i].astype(BF16),
                 w_gate_s[i].astype(BF16), w_up_s[i].astype(BF16), w_down_s[i].astype(BF16),
                 final_norm_g, final_norm=(i == depth - 1))

    y_prompt = x[:geom.tp].reshape(bp, sp, d)
    y_sample = x[geom.tp:].reshape(bs, ss, d)
    return (y_prompt, y_sample)
```

```python
import functools
import math

import jax
import jax.numpy as jnp
from jax import lax
from jax.experimental import pallas as pl
from jax.experimental.pallas import tpu as pltpu

F32 = jnp.float32
BF16 = jnp.bfloat16
I32 = jnp.int32

EPS = 1e-6
NEG = -1e30

DIL_CONFIGS = ((128, 1), (512, 4), (2048, 16))
HEADS_PER_GROUP = 8
HEAD_DIM = 128
NUM_BUCKETS = 32
MAX_DISTANCE = 1024
POOL_WINDOWS = (2, 4, 8, 16)
TOP_K = 8
N_EXPERT_GROUPS = 8
TOPK_GROUPS = 4
ROUTED_SCALE = 2.5

LANES = 128
ROW_TILE = 512
EXPERT_BLOCK = 256
MOE_TOKENS = 256
ATTN_SUPER = 1024
ATTN_HALF = 64
ATTN_UNROLL = 8
POOL_HALO = 8
VMEM_LIMIT = 56 * 1024 * 1024

NT_DIMS = (((1,), (1,)), ((), ()))


def _dot(a, b):
    return jnp.dot(a, b, preferred_element_type=F32)


def _split_bf16(a):
    hi = a.astype(BF16)
    lo = (a - hi.astype(F32)).astype(BF16)
    return hi, lo


def _norm_mod(x, g, sc, sh):
    ms = jnp.mean(x * x, axis=-1, keepdims=True)
    y = x * lax.rsqrt(ms + EPS) * g
    return y * (1.0 + sc) + sh


def _silu(x):
    return x * jax.nn.sigmoid(x)


class _Geom:
    def __init__(self, bp, sp, bs, ss):
        self.bp, self.sp, self.bs, self.ss = bp, sp, bs, ss
        self.tp, self.ts = bp * sp, bs * ss
        self.t = self.tp + self.ts
        self.nb = bp + bs

    def seq_of_row(self, row):
        return jnp.where(row < self.tp, row // self.sp, self.bp + (row - self.tp) // self.ss)

    def pos_and_len(self, row):
        in_p = row < self.tp
        pos = jnp.where(in_p, row % self.sp, (row - self.tp) % self.ss)
        return pos, jnp.where(in_p, self.sp, self.ss)


def _mod_kernel(c_ref, w_ref, b_ref, o_ref):
    a_hi, a_lo = _split_bf16(_silu(c_ref[...]))
    w_hi, w_lo = _split_bf16(w_ref[...])
    o_ref[...] = _dot(a_hi, w_hi) + _dot(a_hi, w_lo) + _dot(a_lo, w_hi) + b_ref[...]


def _modulation(c, w_mod, b_mod):
    depth, d, n = w_mod.shape
    nb = c.shape[0]
    tn = 1024
    return pl.pallas_call(
        _mod_kernel,
        out_shape=jax.ShapeDtypeStruct((depth, nb, n), F32),
        grid=(depth, n // tn),
        in_specs=[
            pl.BlockSpec((nb, d), lambda l, j: (0, 0)),
            pl.BlockSpec((None, d, tn), lambda l, j: (l, 0, j)),
            pl.BlockSpec((None, 1, tn), lambda l, j: (l, 0, j)),
        ],
        out_specs=pl.BlockSpec((None, nb, tn), lambda l, j: (l, 0, j)),
        compiler_params=pltpu.CompilerParams(vmem_limit_bytes=VMEM_LIMIT),
        name="modulation",
    )(c, w_mod, b_mod.reshape(depth, 1, n))


def _proj_kernel(x_ref, mod_ref, g_ref, w_ref, o_ref, h_ref):
    @pl.when(pl.program_id(1) == 0)
    def _():
        h = _norm_mod(x_ref[...], g_ref[...], mod_ref[1:2, :], mod_ref[0:1, :])
        h_ref[...] = h.astype(BF16)

    o_ref[...] = _dot(h_ref[...], w_ref[...])


def _norm_proj(geom, x, mod4, layer, norm_g, w_bf16, tn):
    t, d = x.shape
    n = w_bf16.shape[1]
    tm = ROW_TILE
    return pl.pallas_call(
        _proj_kernel,
        out_shape=jax.ShapeDtypeStruct((t, n), F32),
        grid=(t // tm, n // tn),
        in_specs=[
            pl.BlockSpec((tm, d), lambda i, j: (i, 0)),
            pl.BlockSpec((None, None, 6, d), lambda i, j: (layer, geom.seq_of_row(i * tm), 0, 0)),
            pl.BlockSpec((1, d), lambda i, j: (0, 0)),
            pl.BlockSpec((d, tn), lambda i, j: (0, j)),
        ],
        out_specs=pl.BlockSpec((tm, tn), lambda i, j: (i, j)),
        scratch_shapes=[pltpu.VMEM((tm, d), BF16)],
        compiler_params=pltpu.CompilerParams(
            dimension_semantics=("arbitrary", "arbitrary"), vmem_limit_bytes=VMEM_LIMIT),
        name="norm_proj",
    )(x, mod4, norm_g.reshape(1, d), w_bf16)


def _qkv_kernel(x_ref, mod_ref, g_ref, w_ref, o_ref, hs_ref, hv_ref, *, cols_per_branch):
    tm = x_ref.shape[0]
    j = pl.program_id(1)

    @pl.when(j == 0)
    def _():
        h = _norm_mod(x_ref[...], g_ref[...], mod_ref[1:2, :], mod_ref[0:1, :])
        n_lane_blocks = hs_ref.shape[0]
        lanes = hs_ref.shape[2]
        for c in range(n_lane_blocks):
            hs_ref[c] = h[:, c * lanes:(c + 1) * lanes]
        for g, (_, d) in enumerate(DIL_CONFIGS):
            if d == 1:
                hv_ref[g] = h.astype(BF16)
                continue
            per_class = tm // d
            for r in range(d):
                for c in range(n_lane_blocks):
                    hv_ref[g, pl.ds(r * per_class, per_class), c * lanes:(c + 1) * lanes] = (
                        hs_ref[c, pl.ds(r, per_class, stride=d), :].astype(BF16))

    tn = w_ref.shape[1]
    o_ref[...] = _dot(hv_ref[j // (cols_per_branch // tn)], w_ref[...]).astype(BF16)


def _qkv_proj(geom, x, mod4, layer, norm_g, w_bf16, tn):
    t, d = x.shape
    n = w_bf16.shape[1]
    tm = ATTN_SUPER
    n_br = len(DIL_CONFIGS)
    cols_per_branch = n // n_br
    assert cols_per_branch % tn == 0
    return pl.pallas_call(
        functools.partial(_qkv_kernel, cols_per_branch=cols_per_branch),
        out_shape=jax.ShapeDtypeStruct((t, n), BF16),
        grid=(t // tm, n // tn),
        in_specs=[
            pl.BlockSpec((tm, d), lambda i, j: (i, 0)),
            pl.BlockSpec((None, None, 6, d), lambda i, j: (layer, geom.seq_of_row(i * tm), 0, 0)),
            pl.BlockSpec((1, d), lambda i, j: (0, 0)),
            pl.BlockSpec((d, tn), lambda i, j: (0, j)),
        ],
        out_specs=pl.BlockSpec((tm, tn), lambda i, j: (i, j)),
        scratch_shapes=[pltpu.VMEM((d // LANES, tm, LANES), F32), pltpu.VMEM((n_br, tm, d), BF16)],
        compiler_params=pltpu.CompilerParams(
            dimension_semantics=("arbitrary", "arbitrary"), vmem_limit_bytes=VMEM_LIMIT),
        name="qkv_proj",
    )(x, mod4, norm_g.reshape(1, d), w_bf16)


def _t5_bucket(rel):
    half_b = NUM_BUCKETS // 2
    max_exact = half_b // 2
    n = jnp.abs(rel)
    large = max_exact + (jnp.log(jnp.maximum(n, 1).astype(F32) / max_exact)
                         / math.log(MAX_DISTANCE / max_exact) * (half_b - max_exact)).astype(I32)
    large = jnp.minimum(large, half_b - 1)
    return jnp.where(rel > 0, half_b, 0) + jnp.where(n < max_exact, n, large)


def _attn_bias_tables(rel_bias_table):
    half = ATTN_HALF
    qi = jnp.arange(half)[:, None]
    ki = jnp.arange(3 * half)[None, :]
    off = ki - half - qi
    in_band = jnp.abs(off) <= half
    per_branch = []
    for g, (window, dilation) in enumerate(DIL_CONFIGS):
        assert window // (2 * dilation) == half
        tab = rel_bias_table[:, g * HEADS_PER_GROUP:(g + 1) * HEADS_PER_GROUP]
        b = jnp.transpose(tab[_t5_bucket(off * dilation)], (2, 0, 1)).astype(F32)
        variants = []
        for var in range(4):
            ok = in_band
            if var & 1:
                ok = ok & (ki >= half)
            if var & 2:
                ok = ok & (ki < 2 * half)
            variants.append(jnp.where(ok[None], b, NEG))
        per_branch.append(jnp.stack(variants, axis=1))
    return jnp.stack(per_branch, axis=1)


def _attn_branch(g, d, q_ref, k_ref, v_ref, bias_ref, o_ref, mx_ref, wt_ref, os_ref, ls_ref,
                 sc_ref, pr_ref, dn_ref, lt_ref, seq_len):
    half = ATTN_HALF
    scale = HEAD_DIM ** -0.5
    units = ATTN_SUPER // half
    nbc = units // d
    n_sc = seq_len // ATTN_SUPER
    last = g == len(DIL_CONFIGS) - 1

    def rows(ref, start):
        return ref[pl.ds(pl.multiple_of(start, half), half), :]

    def super_chunk(sc, carry):
        base_sc = pl.multiple_of(sc * ATTN_SUPER, ATTN_SUPER)

        def neighbours(u):
            n = u % nbc
            at_start = n == 0
            at_end = n == nbc - 1
            prev = jnp.where(at_start, base_sc - ATTN_SUPER + (u + nbc - 1) * half, base_sc + (u - 1) * half)
            nxt = jnp.where(at_end, base_sc + ATTN_SUPER + (u - nbc + 1) * half, base_sc + (u + 1) * half)
            no_prev = jnp.logical_and(at_start, sc == 0)
            no_next = jnp.logical_and(at_end, sc == n_sc - 1)
            prev = jnp.where(no_prev, base_sc + u * half, prev)
            nxt = jnp.where(no_next, base_sc + u * half, nxt)
            return prev, nxt, no_prev.astype(I32) + 2 * no_next.astype(I32)

        def window(ref, u):
            prev, nxt, var = neighbours(u)
            return jnp.concatenate([rows(ref, prev), rows(ref, base_sc + u * half), rows(ref, nxt)], axis=0), var

        def score_unit(u, c2):
            kk, var = window(k_ref, u)
            q = rows(q_ref, base_sc + u * half)
            s = lax.dot_general(q, kk, NT_DIMS, preferred_element_type=F32) * scale + bias_ref[g, var]
            sc_ref[pl.ds(pl.multiple_of(u * half, half), half), :] = s
            return c2

        lax.fori_loop(0, units, score_unit, 0, unroll=ATTN_UNROLL)

        s = sc_ref[...]
        m = jnp.max(s, axis=-1, keepdims=True)
        p = jnp.exp(s - m)
        den = jnp.sum(p, axis=-1, keepdims=True)
        pr_ref[...] = p.astype(BF16)
        dn_ref[...] = jnp.broadcast_to(den, dn_ref.shape)
        lt_ref[...] = jnp.broadcast_to(m + jnp.log(den), lt_ref.shape)

        def value_unit(u, c2):
            vv, _ = window(v_ref, u)
            blk = pl.ds(pl.multiple_of(u * half, half), half)
            o = _dot(pr_ref[blk, :], vv) / dn_ref[blk, :]
            nat = pl.ds((u % nbc) * (half * d) + u // nbc, half, stride=d)
            os_ref[nat, :] = o
            ls_ref[nat, :] = lt_ref[blk, :]
            return c2

        lax.fori_loop(0, units, value_unit, 0, unroll=ATTN_UNROLL)

        span = pl.ds(base_sc, ATTN_SUPER)
        og, lg = os_ref[...], ls_ref[...]
        if g == 0:
            o_ref[span, :] = og
            mx_ref[span, :] = lg
            wt_ref[span, :] = jnp.ones_like(lg)
        else:
            m_old = mx_ref[span, :]
            m_new = jnp.maximum(m_old, lg)
            a = jnp.exp(m_old - m_new)
            b = jnp.exp(lg - m_new)
            acc = o_ref[span, :] * a + og * b
            wt = wt_ref[span, :] * a + b
            if last:
                o_ref[span, :] = acc / wt
            else:
                o_ref[span, :] = acc
                mx_ref[span, :] = m_new
                wt_ref[span, :] = wt
        return carry

    lax.fori_loop(0, seq_len // ATTN_SUPER, super_chunk, 0)


def _attn_kernel(q_ref, k_ref, v_ref, bias_ref, *rest, seq_len):
    n_scratch = 8
    o_ref = rest[-n_scratch - 1]
    for g, (_, d) in enumerate(DIL_CONFIGS):
        @pl.when(pl.program_id(2) == g)
        def _(g=g, d=d):
            _attn_branch(g, d, q_ref, k_ref, v_ref, bias_ref, o_ref, *rest[-n_scratch:], seq_len)


def _attention_group(qkv, bias, prev_out, *, n_seq, seq_len, row_block0, t_total):
    assert seq_len % ATTN_SUPER == 0
    for _, d in DIL_CONFIGS:
        assert ATTN_SUPER % (ATTN_HALF * d) == 0
    h = HEADS_PER_GROUP
    width = h * HEAD_DIM
    n_br = len(DIL_CONFIGS)

    def col_spec(c):
        return pl.BlockSpec((seq_len, HEAD_DIM), lambda b, hh, g: (row_block0 + b, (g * 3 + c) * h + hh))

    in_specs = [col_spec(c) for c in range(3)]
    in_specs.append(pl.BlockSpec((None, n_br, 4, ATTN_HALF, 3 * ATTN_HALF),
                                 lambda b, hh, g: (hh, 0, 0, 0, 0)))
    args = [qkv] * 3 + [bias]
    aliases = {}
    if prev_out is not None:
        in_specs.append(pl.BlockSpec(memory_space=pl.ANY))
        args.append(prev_out)
        aliases = {len(args) - 1: 0}
    return pl.pallas_call(
        functools.partial(_attn_kernel, seq_len=seq_len),
        out_shape=jax.ShapeDtypeStruct((t_total, width), F32),
        grid=(n_seq, h, n_br),
        in_specs=in_specs,
        out_specs=pl.BlockSpec((seq_len, HEAD_DIM), lambda b, hh, g: (row_block0 + b, hh)),
        scratch_shapes=[pltpu.VMEM((seq_len, HEAD_DIM), F32),
                        pltpu.VMEM((seq_len, HEAD_DIM), F32),
                        pltpu.VMEM((ATTN_SUPER, HEAD_DIM), F32),
                        pltpu.VMEM((ATTN_SUPER, HEAD_DIM), F32),
                        pltpu.VMEM((ATTN_SUPER, 3 * ATTN_HALF), F32),
                        pltpu.VMEM((ATTN_SUPER, 3 * ATTN_HALF), BF16),
                        pltpu.VMEM((ATTN_SUPER, HEAD_DIM), F32),
                        pltpu.VMEM((ATTN_SUPER, HEAD_DIM), F32)],
        input_output_aliases=aliases,
        compiler_params=pltpu.CompilerParams(
            dimension_semantics=("arbitrary", "arbitrary", "arbitrary"), vmem_limit_bytes=VMEM_LIMIT),
        name="dilated_attention",
    )(*args)


def _route(h2, wrh_ref, wrl_ref, rb_ref, utri_ref, run_ref, te_ref, gt_ref, rk_ref, cnt_ref):
    n_exp = wrh_ref.shape[0]
    tm = h2.shape[0]
    per_group = n_exp // N_EXPERT_GROUPS
    h_hi, h_lo = _split_bf16(h2)
    wrh = wrh_ref[...]
    logits = (lax.dot_general(wrh, h_hi, NT_DIMS, preferred_element_type=F32)
              + lax.dot_general(wrh, h_lo, NT_DIMS, preferred_element_type=F32)
              + lax.dot_general(wrl_ref[...], h_hi, NT_DIMS, preferred_element_type=F32))
    scores = jax.nn.sigmoid(logits)
    biased = scores + rb_ref[...]

    gi = lax.broadcasted_iota(I32, (per_group, tm), 0).astype(F32)
    group_vals, group_scores = [], []
    for g in range(N_EXPERT_GROUPS):
        v = biased[g * per_group:(g + 1) * per_group, :]
        m1 = jnp.max(v, axis=0, keepdims=True)
        i1 = jnp.min(jnp.where(v == m1, gi, float(per_group)), axis=0, keepdims=True)
        m2 = jnp.max(jnp.where(gi == i1, -jnp.inf, v), axis=0, keepdims=True)
        group_vals.append(v)
        group_scores.append(m1 + m2)
    gs = jnp.concatenate(group_scores, axis=0)
    g_iota = lax.broadcasted_iota(I32, gs.shape, 0)
    rank = jnp.zeros(gs.shape, I32)
    for gp in range(N_EXPERT_GROUPS):
        row = gs[gp:gp + 1, :]
        beats = jnp.where(row > gs, 1, jnp.where(row == gs, jnp.where(g_iota > gp, 1, 0), 0))
        rank = rank + beats
    keep = rank < TOPK_GROUPS
    cur = jnp.concatenate(
        [jnp.where(keep[g:g + 1, :], group_vals[g], NEG) for g in range(N_EXPERT_GROUPS)], axis=0)

    e_iota = lax.broadcasted_iota(I32, (n_exp, tm), 0).astype(F32)
    picks, gates = [], []
    for _ in range(TOP_K):
        mk = jnp.max(cur, axis=0, keepdims=True)
        ik = jnp.min(jnp.where(cur == mk, e_iota, float(n_exp)), axis=0, keepdims=True)
        sel = e_iota == ik
        gates.append(jnp.sum(jnp.where(sel, scores, 0.0), axis=0, keepdims=True))
        cur = jnp.where(sel, -jnp.inf, cur)
        picks.append(ik)
    gt = jnp.concatenate(gates, axis=0)
    gt = gt / jnp.sum(gt, axis=0, keepdims=True) * ROUTED_SCALE
    te_ref[...] = jnp.concatenate(picks, axis=0).astype(I32)
    gt_ref[...] = gt

    hot = jnp.zeros((n_exp, tm), F32)
    for k in range(TOP_K):
        hot = hot + jnp.where(e_iota == picks[k], 1.0, 0.0)
    before = _dot(hot.astype(BF16), utri_ref[...]) + run_ref[...]
    ranks = [jnp.sum(jnp.where(e_iota == picks[k], before, 0.0), axis=0, keepdims=True)
             for k in range(TOP_K)]
    rk_ref[...] = jnp.concatenate(ranks, axis=0).astype(I32)
    run_ref[...] = run_ref[...] + jnp.sum(hot, axis=1, keepdims=True)
    cnt_ref[...] = run_ref[...]


def _post_common(mix, x_ref, mod_ref, n2_ref, wrh_ref, wrl_ref, rb_ref, utri_ref,
                 x1_ref, h2_ref, te_ref, gt_ref, rk_ref, cnt_ref, run_ref):
    @pl.when(pl.program_id(0) == 0)
    def _():
        run_ref[...] = jnp.zeros(run_ref.shape, F32)

    x1 = x_ref[...] + mod_ref[2:3, :] * mix
    x1_ref[...] = x1
    h2 = _norm_mod(x1, n2_ref[...], mod_ref[4:5, :], mod_ref[3:4, :])
    h2_ref[...] = h2
    _route(h2, wrh_ref, wrl_ref, rb_ref, utri_ref, run_ref, te_ref, gt_ref, rk_ref, cnt_ref)


def _post_attn_kernel(a_ref, wo_ref, *rest):
    mix = _dot(a_ref[...].astype(BF16), wo_ref[...])
    _post_common(mix, *rest)


def _post_pool_kernel(u_ref, up_ref, un_ref, wg_ref, cs_ref, wo_ref, *rest, geom):
    ext_ref = rest[-1]
    rest = rest[:-1]
    tm, d = u_ref.shape
    halo = POOL_HALO
    row0 = pl.program_id(0) * tm
    pos0, slen = geom.pos_and_len(row0)
    u = u_ref[...]
    ext_ref[pl.ds(halo, tm), :] = u
    ext_ref[pl.ds(0, halo), :] = jnp.where(pos0 > 0, up_ref[...], 0.0)
    ext_ref[pl.ds(halo + tm, halo), :] = jnp.where(pos0 + tm < slen, un_ref[...], 0.0)
    pos = pos0 + lax.broadcasted_iota(I32, (tm, 1), 0)
    gc = d // len(POOL_WINDOWS)
    zs = []
    for g, w in enumerate(POOL_WINDOWS):
        hw = w // 2
        assert hw <= halo
        cols = slice(g * gc, (g + 1) * gc)
        acc = ext_ref[pl.ds(halo - hw, tm), cols]
        for o in range(-hw + 1, hw):
            acc = acc + ext_ref[pl.ds(halo + o, tm), cols]
        cnt = (jnp.minimum(pos + hw, slen) - jnp.maximum(pos - hw, 0)).astype(F32)
        p = acc / cnt - u[:, cols]
        zs.append(_dot(p.astype(BF16), wg_ref[g]) * cs_ref[:, cols])
    z = jnp.concatenate(zs, axis=1)
    mix = _dot(z.astype(BF16), wo_ref[...])
    _post_common(mix, *rest)


def _post_mixer(geom, layer, mixer_args, mixer_specs, body, x, mod4, norm2_g, w_router, router_bias,
                extra_scratch=()):
    t, d = x.shape
    n_exp = w_router.shape[1]
    tm = ROW_TILE
    wr_t = w_router.T
    wr_hi = wr_t.astype(BF16)
    wr_lo = (wr_t - wr_hi.astype(F32)).astype(BF16)
    utri = (jnp.arange(tm)[:, None] < jnp.arange(tm)[None, :]).astype(BF16)
    row = lambda i: (i, 0)
    col = lambda i: (0, i)
    const = lambda i: (0, 0)
    in_specs = list(mixer_specs) + [
        pl.BlockSpec((tm, d), row),
        pl.BlockSpec((None, None, 6, d), lambda i: (layer, geom.seq_of_row(i * tm), 0, 0)),
        pl.BlockSpec((1, d), const),
        pl.BlockSpec((n_exp, d), const),
        pl.BlockSpec((n_exp, d), const),
        pl.BlockSpec((n_exp, 1), const),
        pl.BlockSpec((tm, tm), const),
    ]
    out_shape = [
        jax.ShapeDtypeStruct((t, d), F32),
        jax.ShapeDtypeStruct((t, d), F32),
        jax.ShapeDtypeStruct((TOP_K, t), I32),
        jax.ShapeDtypeStruct((TOP_K, t), F32),
        jax.ShapeDtypeStruct((TOP_K, t), I32),
        jax.ShapeDtypeStruct((n_exp, 1), F32),
    ]
    out_specs = [
        pl.BlockSpec((tm, d), row),
        pl.BlockSpec((tm, d), row),
        pl.BlockSpec((TOP_K, tm), col),
        pl.BlockSpec((TOP_K, tm), col),
        pl.BlockSpec((TOP_K, tm), col),
        pl.BlockSpec((n_exp, 1), const),
    ]
    return pl.pallas_call(
        body,
        out_shape=out_shape,
        grid=(t // tm,),
        in_specs=in_specs,
        out_specs=out_specs,
        scratch_shapes=[pltpu.VMEM((n_exp, 1), F32)] + list(extra_scratch),
        compiler_params=pltpu.CompilerParams(
            dimension_semantics=("arbitrary",), vmem_limit_bytes=VMEM_LIMIT),
        name="post_mixer",
    )(*mixer_args, x, mod4, norm2_g.reshape(1, d), wr_hi, wr_lo, router_bias.reshape(n_exp, 1), utri)


def _dest_kernel(te_ref, rk_ref, ps_ref, d_ref):
    n_exp = ps_ref.shape[0]
    te = te_ref[...]
    e_iota = lax.broadcasted_iota(I32, (n_exp, te.shape[1]), 0)
    ps = ps_ref[...]
    rows = [jnp.sum(jnp.where(e_iota == te[k:k + 1, :], ps, 0.0), axis=0, keepdims=True)
            for k in range(TOP_K)]
    d_ref[...] = jnp.concatenate(rows, axis=0).astype(I32) + rk_ref[...]


def _dest_slots(te, rk, pstart):
    k, t = te.shape
    n_exp = pstart.shape[0]
    tm = ROW_TILE
    col = lambda i: (0, i)
    return pl.pallas_call(
        _dest_kernel,
        out_shape=jax.ShapeDtypeStruct((k, t), I32),
        grid=(t // tm,),
        in_specs=[pl.BlockSpec((k, tm), col), pl.BlockSpec((k, tm), col),
                  pl.BlockSpec((n_exp, 1), lambda i: (0, 0))],
        out_specs=pl.BlockSpec((k, tm), col),
        name="moe_dest",
    )(te, rk, pstart.astype(F32).reshape(n_exp, 1))


def _zero_kernel(lb_ref, o_ref):
    o_ref[...] = jnp.zeros(o_ref.shape, F32)


def _zero_tail_blocks(last_blk, p_rows, d):
    return pl.pallas_call(
        _zero_kernel,
        out_shape=jax.ShapeDtypeStruct((p_rows, d), F32),
        grid_spec=pltpu.PrefetchScalarGridSpec(
            num_scalar_prefetch=1,
            grid=(last_blk.shape[0],),
            in_specs=[],
            out_specs=pl.BlockSpec((EXPERT_BLOCK, d), lambda e, lb: (lb[e], 0)),
        ),
        name="moe_zero_tails",
    )(last_blk)


def _dispatch_kernel(dest_ref, h_ref, xs_in_ref, xs_ref, sem):
    del xs_in_ref
    tt = h_ref.shape[0]

    def issue(t, carry):
        for k in range(TOP_K):
            pltpu.make_async_copy(h_ref.at[pl.ds(t, 1), :],
                                  xs_ref.at[pl.ds(dest_ref[t * TOP_K + k], 1), :], sem).start()
        return carry

    lax.fori_loop(0, tt, issue, 0)
    for k in range(TOP_K):
        pltpu.make_async_copy(h_ref, xs_ref.at[pl.ds(0, tt), :], sem).wait()


def _dispatch(dest, h2, xs_zeroed):
    t, d = h2.shape
    tt = MOE_TOKENS
    return pl.pallas_call(
        _dispatch_kernel,
        out_shape=jax.ShapeDtypeStruct(xs_zeroed.shape, F32),
        grid=(t // tt,),
        in_specs=[
            pl.BlockSpec((TOP_K * tt,), lambda i: (i,), memory_space=pltpu.SMEM),
            pl.BlockSpec((tt, d), lambda i: (i, 0)),
            pl.BlockSpec(memory_space=pl.ANY),
        ],
        out_specs=pl.BlockSpec(memory_space=pl.ANY),
        scratch_shapes=[pltpu.SemaphoreType.DMA(())],
        input_output_aliases={2: 0},
        compiler_params=pltpu.CompilerParams(
            dimension_semantics=("arbitrary",), has_side_effects=True),
        name="moe_dispatch",
    )(dest, h2, xs_zeroed)


def _expert_kernel(be_ref, nu_ref, x_ref, wg_ref, wu_ref, wd_ref, o_ref, wgb_ref, wub_ref, wdb_ref):
    i = pl.program_id(0)

    @pl.when(i < nu_ref[0])
    def _():
        @pl.when(jnp.logical_or(i == 0, be_ref[i] != be_ref[jnp.maximum(i - 1, 0)]))
        def _():
            wgb_ref[...] = wg_ref[...].astype(BF16)
            wub_ref[...] = wu_ref[...].astype(BF16)
            wdb_ref[...] = wd_ref[...].astype(BF16)

        x = x_ref[...].astype(BF16)
        hid = _silu(_dot(x, wgb_ref[...])) * _dot(x, wub_ref[...])
        o_ref[...] = _dot(hid.astype(BF16), wdb_ref[...])


def _experts(layer, blk_expert, n_used, xs, wg, wu, wd):
    p_rows, d = xs.shape
    de = wg.shape[3]
    blk = lambda i, be, nu: (jnp.minimum(i, nu[0] - 1), 0)
    wsel = lambda i, be, nu: (layer, be[i], 0, 0)
    return pl.pallas_call(
        _expert_kernel,
        out_shape=jax.ShapeDtypeStruct((p_rows, d), F32),
        grid_spec=pltpu.PrefetchScalarGridSpec(
            num_scalar_prefetch=2,
            grid=(p_rows // EXPERT_BLOCK,),
            in_specs=[
                pl.BlockSpec((EXPERT_BLOCK, d), blk),
                pl.BlockSpec((None, None, d, de), wsel),
                pl.BlockSpec((None, None, d, de), wsel),
                pl.BlockSpec((None, None, de, d), wsel),
            ],
            out_specs=pl.BlockSpec((EXPERT_BLOCK, d), blk),
            scratch_shapes=[pltpu.VMEM((d, de), BF16), pltpu.VMEM((d, de), BF16), pltpu.VMEM((de, d), BF16)],
        ),
        compiler_params=pltpu.CompilerParams(
            dimension_semantics=("arbitrary",), vmem_limit_bytes=VMEM_LIMIT),
        name="moe_experts",
    )(blk_expert, n_used, xs, wg, wu, wd)


def _combine_kernel(dest_ref, x1_ref, h_ref, gt_ref, mod_ref, wsg_ref, wsu_ref, wsd_ref, fg_ref, ys_ref,
                    o_ref, buf_ref, sem, *, final_norm):
    tt = x1_ref.shape[0]

    def issue(t, carry):
        for k in range(TOP_K):
            pltpu.make_async_copy(ys_ref.at[pl.ds(dest_ref[t * TOP_K + k], 1), :],
                                  buf_ref.at[k, pl.ds(t, 1), :], sem).start()
        return carry

    lax.fori_loop(0, tt, issue, 0)
    hs = h_ref[...].astype(BF16)
    hid = _silu(_dot(hs, wsg_ref[...])) * _dot(hs, wsu_ref[...])
    y = _dot(hid.astype(BF16), wsd_ref[...])
    for k in range(TOP_K):
        pltpu.make_async_copy(ys_ref.at[pl.ds(0, tt), :], buf_ref.at[k], sem).wait()
    gt = gt_ref[...]
    for k in range(TOP_K):
        y = y + buf_ref[k] * gt[:, k:k + 1]
    x2 = x1_ref[...] + mod_ref[5:6, :] * y
    if final_norm:
        ms = jnp.mean(x2 * x2, axis=-1, keepdims=True)
        x2 = x2 * lax.rsqrt(ms + EPS) * fg_ref[...]
    o_ref[...] = x2


def _combine(geom, layer, dest, x1, h2, gate_t, mod4, wsg, wsu, wsd, final_g, ys, final_norm):
    t, d = x1.shape
    de = wsg.shape[1]
    tt = MOE_TOKENS
    row = lambda i: (i, 0)
    const = lambda i: (0, 0)
    return pl.pallas_call(
        functools.partial(_combine_kernel, final_norm=final_norm),
        out_shape=jax.ShapeDtypeStruct((t, d), F32),
        grid=(t // tt,),
        in_specs=[
            pl.BlockSpec((TOP_K * tt,), lambda i: (i,), memory_space=pltpu.SMEM),
            pl.BlockSpec((tt, d), row),
            pl.BlockSpec((tt, d), row),
            pl.BlockSpec((tt, TOP_K), row),
            pl.BlockSpec((None, None, 6, d), lambda i: (layer, geom.seq_of_row(i * tt), 0, 0)),
            pl.BlockSpec((d, de), const),
            pl.BlockSpec((d, de), const),
            pl.BlockSpec((de, d), const),
            pl.BlockSpec((1, d), const),
            pl.BlockSpec(memory_space=pl.ANY),
        ],
        out_specs=pl.BlockSpec((tt, d), row),
        scratch_shapes=[pltpu.VMEM((TOP_K, tt, d), F32), pltpu.SemaphoreType.DMA(())],
        compiler_params=pltpu.CompilerParams(
            dimension_semantics=("arbitrary",), vmem_limit_bytes=VMEM_LIMIT),
        name="moe_combine",
    )(dest, x1, h2, gate_t, mod4, wsg, wsu, wsd, final_g.reshape(1, d), ys)


def _moe(geom, layer, x1, h2, te, gt, rk, cnt, mod4, wg, wu, wd, wsg, wsu, wsd, final_g, final_norm):
    t, d = x1.shape
    n_exp = wg.shape[1]
    p_rows = t * TOP_K + n_exp * EXPERT_BLOCK
    n_blk = p_rows // EXPERT_BLOCK
    counts = cnt[:, 0].astype(I32)
    pcounts = (counts + EXPERT_BLOCK - 1) // EXPERT_BLOCK * EXPERT_BLOCK
    pend = jnp.cumsum(pcounts)
    pstart = pend - pcounts
    n_used = pend[-1] // EXPERT_BLOCK
    blk_ids = jnp.arange(n_blk, dtype=I32)
    be = jnp.sum((pend[None, :] <= (blk_ids * EXPERT_BLOCK)[:, None]).astype(I32), axis=1)
    be = jnp.minimum(be, jnp.minimum(be[n_used - 1], n_exp - 1))
    last_blk = jnp.maximum(pend // EXPERT_BLOCK - 1, 0).astype(I32)

    dest = _dest_slots(te, rk, pstart).T.reshape(-1)
    xs = _dispatch(dest, h2, _zero_tail_blocks(last_blk, p_rows, d))
    ys = _experts(layer, be, n_used.reshape(1).astype(I32), xs, wg, wu, wd)
    return _combine(geom, layer, dest, x1, h2, gt.T, mod4, wsg, wsu, wsd, final_g, ys, final_norm)


def kernel(x_prompt, x_sample, c_prompt, c_sample, w_mod, b_mod, norm1_g, norm2_g, rel_bias_table, w_in_a, w_out_a, w_in_b, w_grp_b, pool_scale_b, w_out_b, w_router, router_bias, w_gate_e, w_up_e, w_down_e, w_gate_s, w_up_s, w_down_s, final_norm_g):
    bp, sp, d = x_prompt.shape
    bs, ss, _ = x_sample.shape
    geom = _Geom(bp, sp, bs, ss)
    depth = w_mod.shape[0]
    tm = ROW_TILE
    assert sp % ATTN_SUPER == 0 and ss % ATTN_SUPER == 0 and geom.tp % ss == 0
    assert sp % tm == 0 and ss % tm == 0 and tm % MOE_TOKENS == 0

    x = jnp.concatenate([x_prompt.reshape(geom.tp, d), x_sample.reshape(geom.ts, d)], axis=0)
    c = jnp.concatenate([c_prompt, c_sample], axis=0)
    mod4 = _modulation(c, w_mod, b_mod).reshape(depth, geom.nb, 6, d)
    bias = _attn_bias_tables(rel_bias_table)

    for i in range(depth):
        j = i // 2
        if i % 2 == 0:
            qkv = _qkv_proj(geom, x, mod4, i, norm1_g[i], w_in_a[j].astype(BF16), 1536)
            attn = _attention_group(qkv, bias, None, n_seq=bp, seq_len=sp, row_block0=0, t_total=geom.t)
            attn = _attention_group(qkv, bias, attn, n_seq=bs, seq_len=ss, row_block0=geom.tp // ss,
                                    t_total=geom.t)
            mixer_args = (attn, w_out_a[j].astype(BF16))
            mixer_specs = (pl.BlockSpec((tm, d), lambda r: (r, 0)),
                           pl.BlockSpec((d, d), lambda r: (0, 0)))
            body, extra = _post_attn_kernel, ()
        else:
            u = _norm_proj(geom, x, mod4, i, norm1_g[i], w_in_b[j].astype(BF16), 1024)
            hb = tm // POOL_HALO
            n_hb = geom.t // POOL_HALO
            mixer_args = (u, u, u, w_grp_b[j].astype(BF16), pool_scale_b[j].reshape(1, d),
                          w_out_b[j].astype(BF16))
            gc = d // len(POOL_WINDOWS)
            mixer_specs = (pl.BlockSpec((tm, d), lambda r: (r, 0)),
                           pl.BlockSpec((POOL_HALO, d), lambda r: (jnp.maximum(r * hb - 1, 0), 0)),
                           pl.BlockSpec((POOL_HALO, d), lambda r: (jnp.minimum((r + 1) * hb, n_hb - 1), 0)),
                           pl.BlockSpec((len(POOL_WINDOWS), gc, gc), lambda r: (0, 0, 0)),
                           pl.BlockSpec((1, d), lambda r: (0, 0)),
                           pl.BlockSpec((d, d), lambda r: (0, 0)))
            body = functools.partial(_post_pool_kernel, geom=geom)
            extra = (pltpu.VMEM((tm + 2 * POOL_HALO, d), F32),)
        x1, h2, te, gt, rk, cnt = _post_mixer(geom, i, mixer_args, mixer_specs, body, x, mod4, norm2_g[i],
                                              w_router[i], router_bias[i], extra)
        x = _moe(geom, i, x1, h2, te, gt, rk, cnt, mod4,
                 w_gate_e, w_up_e, w_down_e,
                 w_gate_s[i].astype(BF16), w_up_s[i].astype(BF16), w_down_s[i].astype(BF16),
                 final_norm_g, final_norm=(i == depth - 1))

    y_prompt = x[:geom.tp].reshape(bp, sp, d)
    y_sample = x[geom.tp:].reshape(bs, ss, d)
    return (y_prompt, y_sample)
```

```python
import functools
import math

import jax
import jax.numpy as jnp
from jax import lax
from jax.experimental import pallas as pl
from jax.experimental.pallas import tpu as pltpu

F32 = jnp.float32
BF16 = jnp.bfloat16
I32 = jnp.int32

EPS = 1e-6
NEG = -1e30

DIL_CONFIGS = ((128, 1), (512, 4), (2048, 16))
HEADS_PER_GROUP = 8
HEAD_DIM = 128
NUM_BUCKETS = 32
MAX_DISTANCE = 1024
POOL_WINDOWS = (2, 4, 8, 16)
TOP_K = 8
N_EXPERT_GROUPS = 8
TOPK_GROUPS = 4
ROUTED_SCALE = 2.5

LANES = 128
ROW_TILE = 512
EXPERT_BLOCK = 256
MOE_TOKENS = 256
ATTN_SUPER = 1024
ATTN_HALF = 64
ATTN_UNROLL = 8
POOL_HALO = 8
VMEM_LIMIT = 56 * 1024 * 1024

NT_DIMS = (((1,), (1,)), ((), ()))


def _dot(a, b):
    return jnp.dot(a, b, preferred_element_type=F32)


def _split_bf16(a):
    hi = a.astype(BF16)
    lo = (a - hi.astype(F32)).astype(BF16)
    return hi, lo


def _norm_mod(x, g, sc, sh):
    ms = jnp.mean(x * x, axis=-1, keepdims=True)
    y = x * lax.rsqrt(ms + EPS) * g
    return y * (1.0 + sc) + sh


def _silu(x):
    return x * jax.nn.sigmoid(x)


U32 = jnp.uint32
_HI16 = 0xFFFF0000


def _pack_bf16_pair(x):
    n = x.shape[1] // 2
    r = lax.bitcast_convert_type(x.astype(BF16).astype(F32), U32)
    return lax.shift_right_logical(r[:, :n], jnp.uint32(16)) | (r[:, n:] & jnp.uint32(_HI16))


def _unpack_bf16_pair(p):
    lo = lax.bitcast_convert_type(lax.shift_left(p, jnp.uint32(16)), F32)
    hi = lax.bitcast_convert_type(p & jnp.uint32(_HI16), F32)
    return lo, hi


class _Geom:
    def __init__(self, bp, sp, bs, ss):
        self.bp, self.sp, self.bs, self.ss = bp, sp, bs, ss
        self.tp, self.ts = bp * sp, bs * ss
        self.t = self.tp + self.ts
        self.nb = bp + bs

    def seq_of_row(self, row):
        return jnp.where(row < self.tp, row // self.sp, self.bp + (row - self.tp) // self.ss)

    def pos_and_len(self, row):
        in_p = row < self.tp
        pos = jnp.where(in_p, row % self.sp, (row - self.tp) % self.ss)
        return pos, jnp.where(in_p, self.sp, self.ss)


def _mod_kernel(c_ref, w_ref, b_ref, o_ref):
    a_hi, a_lo = _split_bf16(_silu(c_ref[...]))
    w_hi, w_lo = _split_bf16(w_ref[...])
    o_ref[...] = _dot(a_hi, w_hi) + _dot(a_hi, w_lo) + _dot(a_lo, w_hi) + b_ref[...]


def _modulation(c, w_mod, b_mod):
    depth, d, n = w_mod.shape
    nb = c.shape[0]
    tn = 1024
    return pl.pallas_call(
        _mod_kernel,
        out_shape=jax.ShapeDtypeStruct((depth, nb, n), F32),
        grid=(depth, n // tn),
        in_specs=[
            pl.BlockSpec((nb, d), lambda l, j: (0, 0)),
            pl.BlockSpec((None, d, tn), lambda l, j: (l, 0, j)),
            pl.BlockSpec((None, 1, tn), lambda l, j: (l, 0, j)),
        ],
        out_specs=pl.BlockSpec((None, nb, tn), lambda l, j: (l, 0, j)),
        compiler_params=pltpu.CompilerParams(vmem_limit_bytes=VMEM_LIMIT),
        name="modulation",
    )(c, w_mod, b_mod.reshape(depth, 1, n))


def _proj_kernel(x_ref, mod_ref, g_ref, w_ref, o_ref, h_ref):
    @pl.when(pl.program_id(1) == 0)
    def _():
        h = _norm_mod(x_ref[...], g_ref[...], mod_ref[1:2, :], mod_ref[0:1, :])
        h_ref[...] = h.astype(BF16)

    o_ref[...] = _dot(h_ref[...], w_ref[...])


def _norm_proj(geom, x, mod4, layer, norm_g, w_bf16, tn):
    t, d = x.shape
    n = w_bf16.shape[1]
    tm = ROW_TILE
    return pl.pallas_call(
        _proj_kernel,
        out_shape=jax.ShapeDtypeStruct((t, n), F32),
        grid=(t // tm, n // tn),
        in_specs=[
            pl.BlockSpec((tm, d), lambda i, j: (i, 0)),
            pl.BlockSpec((None, None, 6, d), lambda i, j: (layer, geom.seq_of_row(i * tm), 0, 0)),
            pl.BlockSpec((1, d), lambda i, j: (0, 0)),
            pl.BlockSpec((d, tn), lambda i, j: (0, j)),
        ],
        out_specs=pl.BlockSpec((tm, tn), lambda i, j: (i, j)),
        scratch_shapes=[pltpu.VMEM((tm, d), BF16)],
        compiler_params=pltpu.CompilerParams(
            dimension_semantics=("arbitrary", "arbitrary"), vmem_limit_bytes=VMEM_LIMIT),
        name="norm_proj",
    )(x, mod4, norm_g.reshape(1, d), w_bf16)


def _qkv_kernel(x_ref, mod_ref, g_ref, w_ref, o_ref, hs_ref, hv_ref, *, cols_per_branch):
    tm = x_ref.shape[0]
    j = pl.program_id(1)

    @pl.when(j == 0)
    def _():
        h = _norm_mod(x_ref[...], g_ref[...], mod_ref[1:2, :], mod_ref[0:1, :])
        n_lane_blocks = hs_ref.shape[0]
        lanes = hs_ref.shape[2]
        for c in range(n_lane_blocks):
            hs_ref[c] = h[:, c * lanes:(c + 1) * lanes]
        for g, (_, d) in enumerate(DIL_CONFIGS):
            if d == 1:
                hv_ref[g] = h.astype(BF16)
                continue
            per_class = tm // d
            for r in range(d):
                for c in range(n_lane_blocks):
                    hv_ref[g, pl.ds(r * per_class, per_class), c * lanes:(c + 1) * lanes] = (
                        hs_ref[c, pl.ds(r, per_class, stride=d), :].astype(BF16))

    tn = w_ref.shape[1]
    o_ref[...] = _dot(hv_ref[j // (cols_per_branch // tn)], w_ref[...]).astype(BF16)


def _qkv_proj(geom, x, mod4, layer, norm_g, w_bf16, tn):
    t, d = x.shape
    n = w_bf16.shape[1]
    tm = ATTN_SUPER
    n_br = len(DIL_CONFIGS)
    cols_per_branch = n // n_br
    assert cols_per_branch % tn == 0
    return pl.pallas_call(
        functools.partial(_qkv_kernel, cols_per_branch=cols_per_branch),
        out_shape=jax.ShapeDtypeStruct((t, n), BF16),
        grid=(t // tm, n // tn),
        in_specs=[
            pl.BlockSpec((tm, d), lambda i, j: (i, 0)),
            pl.BlockSpec((None, None, 6, d), lambda i, j: (layer, geom.seq_of_row(i * tm), 0, 0)),
            pl.BlockSpec((1, d), lambda i, j: (0, 0)),
            pl.BlockSpec((d, tn), lambda i, j: (0, j)),
        ],
        out_specs=pl.BlockSpec((tm, tn), lambda i, j: (i, j)),
        scratch_shapes=[pltpu.VMEM((d // LANES, tm, LANES), F32), pltpu.VMEM((n_br, tm, d), BF16)],
        compiler_params=pltpu.CompilerParams(
            dimension_semantics=("arbitrary", "arbitrary"), vmem_limit_bytes=VMEM_LIMIT),
        name="qkv_proj",
    )(x, mod4, norm_g.reshape(1, d), w_bf16)


def _t5_bucket(rel):
    half_b = NUM_BUCKETS // 2
    max_exact = half_b // 2
    n = jnp.abs(rel)
    large = max_exact + (jnp.log(jnp.maximum(n, 1).astype(F32) / max_exact)
                         / math.log(MAX_DISTANCE / max_exact) * (half_b - max_exact)).astype(I32)
    large = jnp.minimum(large, half_b - 1)
    return jnp.where(rel > 0, half_b, 0) + jnp.where(n < max_exact, n, large)


def _attn_bias_tables(rel_bias_table):
    half = ATTN_HALF
    qi = jnp.arange(half)[:, None]
    ki = jnp.arange(3 * half)[None, :]
    off = ki - half - qi
    in_band = jnp.abs(off) <= half
    per_branch = []
    for g, (window, dilation) in enumerate(DIL_CONFIGS):
        assert window // (2 * dilation) == half
        tab = rel_bias_table[:, g * HEADS_PER_GROUP:(g + 1) * HEADS_PER_GROUP]
        b = jnp.transpose(tab[_t5_bucket(off * dilation)], (2, 0, 1)).astype(F32)
        variants = []
        for var in range(4):
            ok = in_band
            if var & 1:
                ok = ok & (ki >= half)
            if var & 2:
                ok = ok & (ki < 2 * half)
            variants.append(jnp.where(ok[None], b, NEG))
        per_branch.append(jnp.stack(variants, axis=1))
    return jnp.stack(per_branch, axis=1)


def _attn_branch(g, d, q_ref, k_ref, v_ref, bias_ref, o_ref, mx_ref, wt_ref, os_ref, ls_ref,
                 sc_ref, pr_ref, dn_ref, lt_ref, seq_len):
    half = ATTN_HALF
    scale = HEAD_DIM ** -0.5
    units = ATTN_SUPER // half
    nbc = units // d
    n_sc = seq_len // ATTN_SUPER
    last = g == len(DIL_CONFIGS) - 1

    def rows(ref, start):
        return ref[pl.ds(pl.multiple_of(start, half), half), :]

    def super_chunk(sc, carry):
        base_sc = pl.multiple_of(sc * ATTN_SUPER, ATTN_SUPER)

        def neighbours(u):
            n = u % nbc
            at_start = n == 0
            at_end = n == nbc - 1
            prev = jnp.where(at_start, base_sc - ATTN_SUPER + (u + nbc - 1) * half, base_sc + (u - 1) * half)
            nxt = jnp.where(at_end, base_sc + ATTN_SUPER + (u - nbc + 1) * half, base_sc + (u + 1) * half)
            no_prev = jnp.logical_and(at_start, sc == 0)
            no_next = jnp.logical_and(at_end, sc == n_sc - 1)
            prev = jnp.where(no_prev, base_sc + u * half, prev)
            nxt = jnp.where(no_next, base_sc + u * half, nxt)
            return prev, nxt, no_prev.astype(I32) + 2 * no_next.astype(I32)

        def window(ref, u):
            prev, nxt, var = neighbours(u)
            return jnp.concatenate([rows(ref, prev), rows(ref, base_sc + u * half), rows(ref, nxt)], axis=0), var

        def score_unit(u, c2):
            kk, var = window(k_ref, u)
            q = rows(q_ref, base_sc + u * half)
            s = lax.dot_general(q, kk, NT_DIMS, preferred_element_type=F32) * scale + bias_ref[g, var]
            sc_ref[pl.ds(pl.multiple_of(u * half, half), half), :] = s
            return c2

        lax.fori_loop(0, units, score_unit, 0, unroll=ATTN_UNROLL)

        s = sc_ref[...]
        m = jnp.max(s, axis=-1, keepdims=True)
        p = jnp.exp(s - m)
        den = jnp.sum(p, axis=-1, keepdims=True)
        pr_ref[...] = p.astype(BF16)
        dn_ref[...] = jnp.broadcast_to(den, dn_ref.shape)
        lt_ref[...] = jnp.broadcast_to(m + jnp.log(den), lt_ref.shape)

        def value_unit(u, c2):
            vv, _ = window(v_ref, u)
            blk = pl.ds(pl.multiple_of(u * half, half), half)
            o = _dot(pr_ref[blk, :], vv) / dn_ref[blk, :]
            nat = pl.ds((u % nbc) * (half * d) + u // nbc, half, stride=d)
            os_ref[nat, :] = o
            ls_ref[nat, :] = lt_ref[blk, :]
            return c2

        lax.fori_loop(0, units, value_unit, 0, unroll=ATTN_UNROLL)

        span = pl.ds(base_sc, ATTN_SUPER)
        og, lg = os_ref[...], ls_ref[...]
        if g == 0:
            o_ref[span, :] = og
            mx_ref[span, :] = lg
            wt_ref[span, :] = jnp.ones_like(lg)
        else:
            m_old = mx_ref[span, :]
            m_new = jnp.maximum(m_old, lg)
            a = jnp.exp(m_old - m_new)
            b = jnp.exp(lg - m_new)
            acc = o_ref[span, :] * a + og * b
            wt = wt_ref[span, :] * a + b
            if last:
                o_ref[span, :] = acc / wt
            else:
                o_ref[span, :] = acc
                mx_ref[span, :] = m_new
                wt_ref[span, :] = wt
        return carry

    lax.fori_loop(0, seq_len // ATTN_SUPER, super_chunk, 0)


def _attn_kernel(q_ref, k_ref, v_ref, bias_ref, *rest, seq_len):
    n_scratch = 8
    o_ref = rest[-n_scratch - 1]
    for g, (_, d) in enumerate(DIL_CONFIGS):
        @pl.when(pl.program_id(2) == g)
        def _(g=g, d=d):
            _attn_branch(g, d, q_ref, k_ref, v_ref, bias_ref, o_ref, *rest[-n_scratch:], seq_len)


def _attention_group(qkv, bias, prev_out, *, n_seq, seq_len, row_block0, t_total):
    assert seq_len % ATTN_SUPER == 0
    for _, d in DIL_CONFIGS:
        assert ATTN_SUPER % (ATTN_HALF * d) == 0
    h = HEADS_PER_GROUP
    width = h * HEAD_DIM
    n_br = len(DIL_CONFIGS)

    def col_spec(c):
        return pl.BlockSpec((seq_len, HEAD_DIM), lambda b, hh, g: (row_block0 + b, (g * 3 + c) * h + hh))

    in_specs = [col_spec(c) for c in range(3)]
    in_specs.append(pl.BlockSpec((None, n_br, 4, ATTN_HALF, 3 * ATTN_HALF),
                                 lambda b, hh, g: (hh, 0, 0, 0, 0)))
    args = [qkv] * 3 + [bias]
    aliases = {}
    if prev_out is not None:
        in_specs.append(pl.BlockSpec(memory_space=pl.ANY))
        args.append(prev_out)
        aliases = {len(args) - 1: 0}
    return pl.pallas_call(
        functools.partial(_attn_kernel, seq_len=seq_len),
        out_shape=jax.ShapeDtypeStruct((t_total, width), F32),
        grid=(n_seq, h, n_br),
        in_specs=in_specs,
        out_specs=pl.BlockSpec((seq_len, HEAD_DIM), lambda b, hh, g: (row_block0 + b, hh)),
        scratch_shapes=[pltpu.VMEM((seq_len, HEAD_DIM), F32),
                        pltpu.VMEM((seq_len, HEAD_DIM), F32),
                        pltpu.VMEM((ATTN_SUPER, HEAD_DIM), F32),
                        pltpu.VMEM((ATTN_SUPER, HEAD_DIM), F32),
                        pltpu.VMEM((ATTN_SUPER, 3 * ATTN_HALF), F32),
                        pltpu.VMEM((ATTN_SUPER, 3 * ATTN_HALF), BF16),
                        pltpu.VMEM((ATTN_SUPER, HEAD_DIM), F32),
                        pltpu.VMEM((ATTN_SUPER, HEAD_DIM), F32)],
        input_output_aliases=aliases,
        compiler_params=pltpu.CompilerParams(
            dimension_semantics=("arbitrary", "arbitrary", "arbitrary"), vmem_limit_bytes=VMEM_LIMIT),
        name="dilated_attention",
    )(*args)


def _route(h2, wrh_ref, wrl_ref, rb_ref, utri_ref, run_ref, te_ref, gt_ref, rk_ref, cnt_ref):
    n_exp = wrh_ref.shape[0]
    tm = h2.shape[0]
    per_group = n_exp // N_EXPERT_GROUPS
    h_hi, h_lo = _split_bf16(h2)
    wrh = wrh_ref[...]
    logits = (lax.dot_general(wrh, h_hi, NT_DIMS, preferred_element_type=F32)
              + lax.dot_general(wrh, h_lo, NT_DIMS, preferred_element_type=F32)
              + lax.dot_general(wrl_ref[...], h_hi, NT_DIMS, preferred_element_type=F32))
    scores = jax.nn.sigmoid(logits)
    biased = scores + rb_ref[...]

    gi = lax.broadcasted_iota(I32, (per_group, tm), 0).astype(F32)
    group_vals, group_scores = [], []
    for g in range(N_EXPERT_GROUPS):
        v = biased[g * per_group:(g + 1) * per_group, :]
        m1 = jnp.max(v, axis=0, keepdims=True)
        i1 = jnp.min(jnp.where(v == m1, gi, float(per_group)), axis=0, keepdims=True)
        m2 = jnp.max(jnp.where(gi == i1, -jnp.inf, v), axis=0, keepdims=True)
        group_vals.append(v)
        group_scores.append(m1 + m2)
    gs = jnp.concatenate(group_scores, axis=0)
    g_iota = lax.broadcasted_iota(I32, gs.shape, 0)
    rank = jnp.zeros(gs.shape, I32)
    for gp in range(N_EXPERT_GROUPS):
        row = gs[gp:gp + 1, :]
        beats = jnp.where(row > gs, 1, jnp.where(row == gs, jnp.where(g_iota > gp, 1, 0), 0))
        rank = rank + beats
    keep = rank < TOPK_GROUPS
    cur = jnp.concatenate(
        [jnp.where(keep[g:g + 1, :], group_vals[g], NEG) for g in range(N_EXPERT_GROUPS)], axis=0)

    e_iota = lax.broadcasted_iota(I32, (n_exp, tm), 0).astype(F32)
    picks, gates = [], []
    for _ in range(TOP_K):
        mk = jnp.max(cur, axis=0, keepdims=True)
        ik = jnp.min(jnp.where(cur == mk, e_iota, float(n_exp)), axis=0, keepdims=True)
        sel = e_iota == ik
        gates.append(jnp.sum(jnp.where(sel, scores, 0.0), axis=0, keepdims=True))
        cur = jnp.where(sel, -jnp.inf, cur)
        picks.append(ik)
    gt = jnp.concatenate(gates, axis=0)
    gt = gt / jnp.sum(gt, axis=0, keepdims=True) * ROUTED_SCALE
    te_ref[...] = jnp.concatenate(picks, axis=0).astype(I32)
    gt_ref[...] = gt

    hot = jnp.zeros((n_exp, tm), F32)
    for k in range(TOP_K):
        hot = hot + jnp.where(e_iota == picks[k], 1.0, 0.0)
    before = _dot(hot.astype(BF16), utri_ref[...]) + run_ref[...]
    ranks = [jnp.sum(jnp.where(e_iota == picks[k], before, 0.0), axis=0, keepdims=True)
             for k in range(TOP_K)]
    rk_ref[...] = jnp.concatenate(ranks, axis=0).astype(I32)
    run_ref[...] = run_ref[...] + jnp.sum(hot, axis=1, keepdims=True)
    cnt_ref[...] = run_ref[...]


def _post_common(mix, x_ref, mod_ref, n2_ref, wrh_ref, wrl_ref, rb_ref, utri_ref,
                 x1_ref, h2_ref, te_ref, gt_ref, rk_ref, cnt_ref, run_ref):
    @pl.when(pl.program_id(0) == 0)
    def _():
        run_ref[...] = jnp.zeros(run_ref.shape, F32)

    x1 = x_ref[...] + mod_ref[2:3, :] * mix
    x1_ref[...] = x1
    h2 = _norm_mod(x1, n2_ref[...], mod_ref[4:5, :], mod_ref[3:4, :])
    h2_ref[...] = _pack_bf16_pair(h2)
    _route(h2, wrh_ref, wrl_ref, rb_ref, utri_ref, run_ref, te_ref, gt_ref, rk_ref, cnt_ref)


def _post_attn_kernel(a_ref, wo_ref, *rest):
    mix = _dot(a_ref[...].astype(BF16), wo_ref[...])
    _post_common(mix, *rest)


def _post_pool_kernel(u_ref, up_ref, un_ref, wg_ref, cs_ref, wo_ref, *rest, geom):
    ext_ref = rest[-1]
    rest = rest[:-1]
    tm, d = u_ref.shape
    halo = POOL_HALO
    row0 = pl.program_id(0) * tm
    pos0, slen = geom.pos_and_len(row0)
    u = u_ref[...]
    ext_ref[pl.ds(halo, tm), :] = u
    ext_ref[pl.ds(0, halo), :] = jnp.where(pos0 > 0, up_ref[...], 0.0)
    ext_ref[pl.ds(halo + tm, halo), :] = jnp.where(pos0 + tm < slen, un_ref[...], 0.0)
    pos = pos0 + lax.broadcasted_iota(I32, (tm, 1), 0)
    gc = d // len(POOL_WINDOWS)
    zs = []
    for g, w in enumerate(POOL_WINDOWS):
        hw = w // 2
        assert hw <= halo
        cols = slice(g * gc, (g + 1) * gc)
        acc = ext_ref[pl.ds(halo - hw, tm), cols]
        for o in range(-hw + 1, hw):
            acc = acc + ext_ref[pl.ds(halo + o, tm), cols]
        cnt = (jnp.minimum(pos + hw, slen) - jnp.maximum(pos - hw, 0)).astype(F32)
        p = acc / cnt - u[:, cols]
        zs.append(_dot(p.astype(BF16), wg_ref[g]) * cs_ref[:, cols])
    z = jnp.concatenate(zs, axis=1)
    mix = _dot(z.astype(BF16), wo_ref[...])
    _post_common(mix, *rest)


def _post_mixer(geom, layer, mixer_args, mixer_specs, body, x, mod4, norm2_g, w_router, router_bias,
                extra_scratch=()):
    t, d = x.shape
    n_exp = w_router.shape[1]
    tm = ROW_TILE
    wr_t = w_router.T
    wr_hi = wr_t.astype(BF16)
    wr_lo = (wr_t - wr_hi.astype(F32)).astype(BF16)
    utri = (jnp.arange(tm)[:, None] < jnp.arange(tm)[None, :]).astype(BF16)
    row = lambda i: (i, 0)
    col = lambda i: (0, i)
    const = lambda i: (0, 0)
    in_specs = list(mixer_specs) + [
        pl.BlockSpec((tm, d), row),
        pl.BlockSpec((None, None, 6, d), lambda i: (layer, geom.seq_of_row(i * tm), 0, 0)),
        pl.BlockSpec((1, d), const),
        pl.BlockSpec((n_exp, d), const),
        pl.BlockSpec((n_exp, d), const),
        pl.BlockSpec((n_exp, 1), const),
        pl.BlockSpec((tm, tm), const),
    ]
    out_shape = [
        jax.ShapeDtypeStruct((t, d), F32),
        jax.ShapeDtypeStruct((t, d // 2), U32),
        jax.ShapeDtypeStruct((TOP_K, t), I32),
        jax.ShapeDtypeStruct((TOP_K, t), F32),
        jax.ShapeDtypeStruct((TOP_K, t), I32),
        jax.ShapeDtypeStruct((n_exp, 1), F32),
    ]
    out_specs = [
        pl.BlockSpec((tm, d), row),
        pl.BlockSpec((tm, d // 2), row),
        pl.BlockSpec((TOP_K, tm), col),
        pl.BlockSpec((TOP_K, tm), col),
        pl.BlockSpec((TOP_K, tm), col),
        pl.BlockSpec((n_exp, 1), const),
    ]
    return pl.pallas_call(
        body,
        out_shape=out_shape,
        grid=(t // tm,),
        in_specs=in_specs,
        out_specs=out_specs,
        scratch_shapes=[pltpu.VMEM((n_exp, 1), F32)] + list(extra_scratch),
        compiler_params=pltpu.CompilerParams(
            dimension_semantics=("arbitrary",), vmem_limit_bytes=VMEM_LIMIT),
        name="post_mixer",
    )(*mixer_args, x, mod4, norm2_g.reshape(1, d), wr_hi, wr_lo, router_bias.reshape(n_exp, 1), utri)


def _dest_kernel(te_ref, rk_ref, ps_ref, d_ref):
    n_exp = ps_ref.shape[0]
    te = te_ref[...]
    e_iota = lax.broadcasted_iota(I32, (n_exp, te.shape[1]), 0)
    ps = ps_ref[...]
    rows = [jnp.sum(jnp.where(e_iota == te[k:k + 1, :], ps, 0.0), axis=0, keepdims=True)
            for k in range(TOP_K)]
    d_ref[...] = jnp.concatenate(rows, axis=0).astype(I32) + rk_ref[...]


def _dest_slots(te, rk, pstart):
    k, t = te.shape
    n_exp = pstart.shape[0]
    tm = ROW_TILE
    col = lambda i: (0, i)
    return pl.pallas_call(
        _dest_kernel,
        out_shape=jax.ShapeDtypeStruct((k, t), I32),
        grid=(t // tm,),
        in_specs=[pl.BlockSpec((k, tm), col), pl.BlockSpec((k, tm), col),
                  pl.BlockSpec((n_exp, 1), lambda i: (0, 0))],
        out_specs=pl.BlockSpec((k, tm), col),
        name="moe_dest",
    )(te, rk, pstart.astype(F32).reshape(n_exp, 1))


def _zero_kernel(lb_ref, o_ref):
    o_ref[...] = jnp.zeros(o_ref.shape, o_ref.dtype)


def _zero_tail_blocks(last_blk, p_rows, d):
    return pl.pallas_call(
        _zero_kernel,
        out_shape=jax.ShapeDtypeStruct((p_rows, d), U32),
        grid_spec=pltpu.PrefetchScalarGridSpec(
            num_scalar_prefetch=1,
            grid=(last_blk.shape[0],),
            in_specs=[],
            out_specs=pl.BlockSpec((EXPERT_BLOCK, d), lambda e, lb: (lb[e], 0)),
        ),
        name="moe_zero_tails",
    )(last_blk)


def _dispatch_kernel(dest_ref, h_ref, xs_in_ref, xs_ref, sem):
    del xs_in_ref
    tt = h_ref.shape[0]

    def issue(t, carry):
        for k in range(TOP_K):
            pltpu.make_async_copy(h_ref.at[pl.ds(t, 1), :],
                                  xs_ref.at[pl.ds(dest_ref[t * TOP_K + k], 1), :], sem).start()
        return carry

    lax.fori_loop(0, tt, issue, 0)
    for k in range(TOP_K):
        pltpu.make_async_copy(h_ref, xs_ref.at[pl.ds(0, tt), :], sem).wait()


def _dispatch(dest, h2, xs_zeroed):
    t, d = h2.shape
    tt = MOE_TOKENS
    return pl.pallas_call(
        _dispatch_kernel,
        out_shape=jax.ShapeDtypeStruct(xs_zeroed.shape, xs_zeroed.dtype),
        grid=(t // tt,),
        in_specs=[
            pl.BlockSpec((TOP_K * tt,), lambda i: (i,), memory_space=pltpu.SMEM),
            pl.BlockSpec((tt, d), lambda i: (i, 0)),
            pl.BlockSpec(memory_space=pl.ANY),
        ],
        out_specs=pl.BlockSpec(memory_space=pl.ANY),
        scratch_shapes=[pltpu.SemaphoreType.DMA(())],
        input_output_aliases={2: 0},
        compiler_params=pltpu.CompilerParams(
            dimension_semantics=("arbitrary",), has_side_effects=True),
        name="moe_dispatch",
    )(dest, h2, xs_zeroed)


def _expert_kernel(be_ref, nu_ref, x_ref, wg_ref, wu_ref, wd_ref, o_ref, wgb_ref, wub_ref, wdb_ref):
    i = pl.program_id(0)

    @pl.when(i < nu_ref[0])
    def _():
        @pl.when(jnp.logical_or(i == 0, be_ref[i] != be_ref[jnp.maximum(i - 1, 0)]))
        def _():
            wgb_ref[...] = wg_ref[...].astype(BF16)
            wub_ref[...] = wu_ref[...].astype(BF16)
            wdb_ref[...] = wd_ref[...].astype(BF16)

        half = x_ref.shape[1]
        lo, hi = _unpack_bf16_pair(x_ref[...])
        lo, hi = lo.astype(BF16), hi.astype(BF16)

        def proj(w_ref):
            return _dot(lo, w_ref[pl.ds(0, half), :]) + _dot(hi, w_ref[pl.ds(half, half), :])

        hid = _silu(proj(wgb_ref)) * proj(wub_ref)
        o_ref[...] = _pack_bf16_pair(_dot(hid.astype(BF16), wdb_ref[...]))


def _experts(layer, blk_expert, n_used, xs, wg, wu, wd):
    p_rows, dp = xs.shape
    d = 2 * dp
    de = wg.shape[3]
    blk = lambda i, be, nu: (jnp.minimum(i, nu[0] - 1), 0)
    wsel = lambda i, be, nu: (layer, be[i], 0, 0)
    return pl.pallas_call(
        _expert_kernel,
        out_shape=jax.ShapeDtypeStruct((p_rows, dp), U32),
        grid_spec=pltpu.PrefetchScalarGridSpec(
            num_scalar_prefetch=2,
            grid=(p_rows // EXPERT_BLOCK,),
            in_specs=[
                pl.BlockSpec((EXPERT_BLOCK, dp), blk),
                pl.BlockSpec((None, None, d, de), wsel),
                pl.BlockSpec((None, None, d, de), wsel),
                pl.BlockSpec((None, None, de, d), wsel),
            ],
            out_specs=pl.BlockSpec((EXPERT_BLOCK, dp), blk),
            scratch_shapes=[pltpu.VMEM((d, de), BF16), pltpu.VMEM((d, de), BF16), pltpu.VMEM((de, d), BF16)],
        ),
        compiler_params=pltpu.CompilerParams(
            dimension_semantics=("arbitrary",), vmem_limit_bytes=VMEM_LIMIT),
        name="moe_experts",
    )(blk_expert, n_used, xs, wg, wu, wd)


def _combine_kernel(dest_ref, dnext_ref, x1_ref, h_ref, gt_ref, mod_ref, wsg_ref, wsu_ref, wsd_ref, fg_ref,
                    ys_ref, o_ref, buf_ref, sems, *, final_norm):
    tt = x1_ref.shape[0]
    half = h_ref.shape[1]
    i = pl.program_id(0)
    slot = i % 2

    def issue(d_ref, s):
        def body(t, carry):
            for k in range(TOP_K):
                pltpu.make_async_copy(ys_ref.at[pl.ds(d_ref[t * TOP_K + k], 1), :],
                                      buf_ref.at[s, k, pl.ds(t, 1), :], sems.at[s]).start()
            return carry

        lax.fori_loop(0, tt, body, 0)

    @pl.when(i == 0)
    def _():
        issue(dest_ref, 0)

    for s in range(2):
        @pl.when(jnp.logical_and(i + 1 < pl.num_programs(0), slot == 1 - s))
        def _(s=s):
            issue(dnext_ref, s)

    lo, hi = _unpack_bf16_pair(h_ref[...])
    lo, hi = lo.astype(BF16), hi.astype(BF16)

    def proj(w_ref):
        return _dot(lo, w_ref[pl.ds(0, half), :]) + _dot(hi, w_ref[pl.ds(half, half), :])

    hid = _silu(proj(wsg_ref)) * proj(wsu_ref)
    y = _dot(hid.astype(BF16), wsd_ref[...])
    for k in range(TOP_K):
        pltpu.make_async_copy(ys_ref.at[pl.ds(0, tt), :], buf_ref.at[slot, k], sems.at[slot]).wait()
    gt = gt_ref[...]
    acc_lo = jnp.zeros((tt, half), F32)
    acc_hi = jnp.zeros((tt, half), F32)
    for k in range(TOP_K):
        e_lo, e_hi = _unpack_bf16_pair(buf_ref[slot, k])
        acc_lo = acc_lo + e_lo * gt[:, k:k + 1]
        acc_hi = acc_hi + e_hi * gt[:, k:k + 1]
    y = y + jnp.concatenate([acc_lo, acc_hi], axis=1)
    x2 = x1_ref[...] + mod_ref[5:6, :] * y
    if final_norm:
        ms = jnp.mean(x2 * x2, axis=-1, keepdims=True)
        x2 = x2 * lax.rsqrt(ms + EPS) * fg_ref[...]
    o_ref[...] = x2


def _combine(geom, layer, dest, x1, h2, gate_t, mod4, wsg, wsu, wsd, final_g, ys, final_norm):
    t, d = x1.shape
    de = wsg.shape[1]
    tt = MOE_TOKENS
    n_steps = t // tt
    row = lambda i: (i, 0)
    const = lambda i: (0, 0)
    return pl.pallas_call(
        functools.partial(_combine_kernel, final_norm=final_norm),
        out_shape=jax.ShapeDtypeStruct((t, d), F32),
        grid=(n_steps,),
        in_specs=[
            pl.BlockSpec((TOP_K * tt,), lambda i: (i,), memory_space=pltpu.SMEM),
            pl.BlockSpec((TOP_K * tt,), lambda i: (jnp.minimum(i + 1, n_steps - 1),), memory_space=pltpu.SMEM),
            pl.BlockSpec((tt, d), row),
            pl.BlockSpec((tt, d // 2), row),
            pl.BlockSpec((tt, TOP_K), row),
            pl.BlockSpec((None, None, 6, d), lambda i: (layer, geom.seq_of_row(i * tt), 0, 0)),
            pl.BlockSpec((d, de), const),
            pl.BlockSpec((d, de), const),
            pl.BlockSpec((de, d), const),
            pl.BlockSpec((1, d), const),
            pl.BlockSpec(memory_space=pl.ANY),
        ],
        out_specs=pl.BlockSpec((tt, d), row),
        scratch_shapes=[pltpu.VMEM((2, TOP_K, tt, d // 2), U32), pltpu.SemaphoreType.DMA((2,))],
        compiler_params=pltpu.CompilerParams(
            dimension_semantics=("arbitrary",), vmem_limit_bytes=VMEM_LIMIT),
        name="moe_combine",
    )(dest, dest, x1, h2, gate_t, mod4, wsg, wsu, wsd, final_g.reshape(1, d), ys)


def _moe(geom, layer, x1, h2, te, gt, rk, cnt, mod4, wg, wu, wd, wsg, wsu, wsd, final_g, final_norm):
    t, d = x1.shape
    n_exp = wg.shape[1]
    p_rows = t * TOP_K + n_exp * EXPERT_BLOCK
    n_blk = p_rows // EXPERT_BLOCK
    counts = cnt[:, 0].astype(I32)
    pcounts = (counts + EXPERT_BLOCK - 1) // EXPERT_BLOCK * EXPERT_BLOCK
    pend = jnp.cumsum(pcounts)
    pstart = pend - pcounts
    n_used = pend[-1] // EXPERT_BLOCK
    blk_ids = jnp.arange(n_blk, dtype=I32)
    be = jnp.sum((pend[None, :] <= (blk_ids * EXPERT_BLOCK)[:, None]).astype(I32), axis=1)
    be = jnp.minimum(be, jnp.minimum(be[n_used - 1], n_exp - 1))
    last_blk = jnp.maximum(pend // EXPERT_BLOCK - 1, 0).astype(I32)

    dest = _dest_slots(te, rk, pstart).T.reshape(-1)
    xs = _dispatch(dest, h2, _zero_tail_blocks(last_blk, p_rows, h2.shape[1]))
    ys = _experts(layer, be, n_used.reshape(1).astype(I32), xs, wg, wu, wd)
    return _combine(geom, layer, dest, x1, h2, gt.T, mod4, wsg, wsu, wsd, final_g, ys, final_norm)


def kernel(x_prompt, x_sample, c_prompt, c_sample, w_mod, b_mod, norm1_g, norm2_g, rel_bias_table, w_in_a, w_out_a, w_in_b, w_grp_b, pool_scale_b, w_out_b, w_router, router_bias, w_gate_e, w_up_e, w_down_e, w_gate_s, w_up_s, w_down_s, final_norm_g):
    bp, sp, d = x_prompt.shape
    bs, ss, _ = x_sample.shape
    geom = _Geom(bp, sp, bs, ss)
    depth = w_mod.shape[0]
    tm = ROW_TILE
    assert sp % ATTN_SUPER == 0 and ss % ATTN_SUPER == 0 and geom.tp % ss == 0
    assert sp % tm == 0 and ss % tm == 0 and tm % MOE_TOKENS == 0

    x = jnp.concatenate([x_prompt.reshape(geom.tp, d), x_sample.reshape(geom.ts, d)], axis=0)
    c = jnp.concatenate([c_prompt, c_sample], axis=0)
    mod4 = _modulation(c, w_mod, b_mod).reshape(depth, geom.nb, 6, d)
    bias = _attn_bias_tables(rel_bias_table)

    for i in range(depth):
        j = i // 2
        if i % 2 == 0:
            qkv = _qkv_proj(geom, x, mod4, i, norm1_g[i], w_in_a[j].astype(BF16), 1536)
            attn = _attention_group(qkv, bias, None, n_seq=bp, seq_len=sp, row_block0=0, t_total=geom.t)
            attn = _attention_group(qkv, bias, attn, n_seq=bs, seq_len=ss, row_block0=geom.tp // ss,
                                    t_total=geom.t)
            mixer_args = (attn, w_out_a[j].astype(BF16))
            mixer_specs = (pl.BlockSpec((tm, d), lambda r: (r, 0)),
                           pl.BlockSpec((d, d), lambda r: (0, 0)))
            body, extra = _post_attn_kernel, ()
        else:
            u = _norm_proj(geom, x, mod4, i, norm1_g[i], w_in_b[j].astype(BF16), 1024)
            hb = tm // POOL_HALO
            n_hb = geom.t // POOL_HALO
            mixer_args = (u, u, u, w_grp_b[j].astype(BF16), pool_scale_b[j].reshape(1, d),
                          w_out_b[j].astype(BF16))
            gc = d // len(POOL_WINDOWS)
            mixer_specs = (pl.BlockSpec((tm, d), lambda r: (r, 0)),
                           pl.BlockSpec((POOL_HALO, d), lambda r: (jnp.maximum(r * hb - 1, 0), 0)),
                           pl.BlockSpec((POOL_HALO, d), lambda r: (jnp.minimum((r + 1) * hb, n_hb - 1), 0)),
                           pl.BlockSpec((len(POOL_WINDOWS), gc, gc), lambda r: (0, 0, 0)),
                           pl.BlockSpec((1, d), lambda r: (0, 0)),
                           pl.BlockSpec((d, d), lambda r: (0, 0)))
            body = functools.partial(_post_pool_kernel, geom=geom)
            extra = (pltpu.VMEM((tm + 2 * POOL_HALO, d), F32),)
        x1, h2, te, gt, rk, cnt = _post_mixer(geom, i, mixer_args, mixer_specs, body, x, mod4, norm2_g[i],
                                              w_router[i], router_bias[i], extra)
        x = _moe(geom, i, x1, h2, te, gt, rk, cnt, mod4,
                 w_gate_e, w_up_e, w_down_e,
                 w_gate_s[i].astype(BF16), w_up_s[i].astype(BF16), w_down_s[i].astype(BF16),
                 final_norm_g, final_norm=(i == depth - 1))

    y_prompt = x[:geom.tp].reshape(bp, sp, d)
    y_sample = x[geom.tp:].reshape(bs, ss, d)
    return (y_prompt, y_sample)
```

```python
import functools
import math

import jax
import jax.numpy as jnp
from jax import lax
from jax.experimental import pallas as pl
from jax.experimental.pallas import tpu as pltpu

F32 = jnp.float32
BF16 = jnp.bfloat16
I32 = jnp.int32

EPS = 1e-6
NEG = -1e30

DIL_CONFIGS = ((128, 1), (512, 4), (2048, 16))
HEADS_PER_GROUP = 8
HEAD_DIM = 128
NUM_BUCKETS = 32
MAX_DISTANCE = 1024
POOL_WINDOWS = (2, 4, 8, 16)
TOP_K = 8
N_EXPERT_GROUPS = 8
TOPK_GROUPS = 4
ROUTED_SCALE = 2.5

LANES = 128
SUBLANES = 8
DMA_QUEUES = 2
ROW_TILE = 512
EXPERT_BLOCK = 512
EXPERT_CHUNK = 256
MOE_TOKENS = 256
ATTN_SUPER = 1024
ATTN_HALF = 64
ATTN_UNROLL = 8
POOL_HALO = 8
VMEM_LIMIT = 56 * 1024 * 1024

NT_DIMS = (((1,), (1,)), ((), ()))


def _dot(a, b):
    return jnp.dot(a, b, preferred_element_type=F32)


def _split_bf16(a):
    hi = a.astype(BF16)
    lo = (a - hi.astype(F32)).astype(BF16)
    return hi, lo


def _norm_mod(x, g, sc, sh):
    ms = jnp.mean(x * x, axis=-1, keepdims=True)
    y = x * lax.rsqrt(ms + EPS) * g
    return y * (1.0 + sc) + sh


def _silu(x):
    return x * jax.nn.sigmoid(x)


U32 = jnp.uint32
_HI16 = 0xFFFF0000


def _pack_bf16_pair(x):
    n = x.shape[1] // 2
    r = lax.bitcast_convert_type(x.astype(BF16).astype(F32), U32)
    return lax.shift_right_logical(r[:, :n], jnp.uint32(16)) | (r[:, n:] & jnp.uint32(_HI16))


def _unpack_bf16_pair(p):
    lo = lax.bitcast_convert_type(lax.shift_left(p, jnp.uint32(16)), F32)
    hi = lax.bitcast_convert_type(p & jnp.uint32(_HI16), F32)
    return lo, hi


class _Geom:
    def __init__(self, bp, sp, bs, ss):
        self.bp, self.sp, self.bs, self.ss = bp, sp, bs, ss
        self.tp, self.ts = bp * sp, bs * ss
        self.t = self.tp + self.ts
        self.nb = bp + bs

    def seq_of_row(self, row):
        return jnp.where(row < self.tp, row // self.sp, self.bp + (row - self.tp) // self.ss)

    def pos_and_len(self, row):
        in_p = row < self.tp
        pos = jnp.where(in_p, row % self.sp, (row - self.tp) % self.ss)
        return pos, jnp.where(in_p, self.sp, self.ss)


def _mod_kernel(c_ref, w_ref, b_ref, o_ref):
    a_hi, a_lo = _split_bf16(_silu(c_ref[...]))
    w_hi, w_lo = _split_bf16(w_ref[...])
    o_ref[...] = _dot(a_hi, w_hi) + _dot(a_hi, w_lo) + _dot(a_lo, w_hi) + b_ref[...]


def _modulation(c, w_mod, b_mod):
    depth, d, n = w_mod.shape
    nb = c.shape[0]
    tn = 1024
    return pl.pallas_call(
        _mod_kernel,
        out_shape=jax.ShapeDtypeStruct((depth, nb, n), F32),
        grid=(depth, n // tn),
        in_specs=[
            pl.BlockSpec((nb, d), lambda l, j: (0, 0)),
            pl.BlockSpec((None, d, tn), lambda l, j: (l, 0, j)),
            pl.BlockSpec((None, 1, tn), lambda l, j: (l, 0, j)),
        ],
        out_specs=pl.BlockSpec((None, nb, tn), lambda l, j: (l, 0, j)),
        compiler_params=pltpu.CompilerParams(vmem_limit_bytes=VMEM_LIMIT),
        name="modulation",
    )(c, w_mod, b_mod.reshape(depth, 1, n))


def _proj_kernel(x_ref, mod_ref, g_ref, w_ref, o_ref, h_ref):
    @pl.when(pl.program_id(1) == 0)
    def _():
        h = _norm_mod(x_ref[...], g_ref[...], mod_ref[1:2, :], mod_ref[0:1, :])
        h_ref[...] = h.astype(BF16)

    o_ref[...] = _dot(h_ref[...], w_ref[...])


def _norm_proj(geom, x, mod4, layer, norm_g, w_bf16, tn):
    t, d = x.shape
    n = w_bf16.shape[1]
    tm = ROW_TILE
    return pl.pallas_call(
        _proj_kernel,
        out_shape=jax.ShapeDtypeStruct((t, n), F32),
        grid=(t // tm, n // tn),
        in_specs=[
            pl.BlockSpec((tm, d), lambda i, j: (i, 0)),
            pl.BlockSpec((None, None, 6, d), lambda i, j: (layer, geom.seq_of_row(i * tm), 0, 0)),
            pl.BlockSpec((1, d), lambda i, j: (0, 0)),
            pl.BlockSpec((d, tn), lambda i, j: (0, j)),
        ],
        out_specs=pl.BlockSpec((tm, tn), lambda i, j: (i, j)),
        scratch_shapes=[pltpu.VMEM((tm, d), BF16)],
        compiler_params=pltpu.CompilerParams(
            dimension_semantics=("arbitrary", "arbitrary"), vmem_limit_bytes=VMEM_LIMIT),
        name="norm_proj",
    )(x, mod4, norm_g.reshape(1, d), w_bf16)


def _qkv_kernel(x_ref, mod_ref, g_ref, w_ref, o_ref, hs_ref, hv_ref, *, cols_per_branch):
    tm = x_ref.shape[0]
    j = pl.program_id(1)

    @pl.when(j == 0)
    def _():
        h = _norm_mod(x_ref[...], g_ref[...], mod_ref[1:2, :], mod_ref[0:1, :])
        n_lane_blocks = hs_ref.shape[0]
        lanes = hs_ref.shape[2]
        for c in range(n_lane_blocks):
            hs_ref[c] = h[:, c * lanes:(c + 1) * lanes]
        for g, (_, d) in enumerate(DIL_CONFIGS):
            if d == 1:
                hv_ref[g] = h.astype(BF16)
                continue
            per_class = tm // d
            for r in range(d):
                for c in range(n_lane_blocks):
                    hv_ref[g, pl.ds(r * per_class, per_class), c * lanes:(c + 1) * lanes] = (
                        hs_ref[c, pl.ds(r, per_class, stride=d), :].astype(BF16))

    tn = w_ref.shape[1]
    o_ref[...] = _dot(hv_ref[j // (cols_per_branch // tn)], w_ref[...]).astype(BF16)


def _qkv_proj(geom, x, mod4, layer, norm_g, w_bf16, tn):
    t, d = x.shape
    n = w_bf16.shape[1]
    tm = ATTN_SUPER
    n_br = len(DIL_CONFIGS)
    cols_per_branch = n // n_br
    assert cols_per_branch % tn == 0
    return pl.pallas_call(
        functools.partial(_qkv_kernel, cols_per_branch=cols_per_branch),
        out_shape=jax.ShapeDtypeStruct((t, n), BF16),
        grid=(t // tm, n // tn),
        in_specs=[
            pl.BlockSpec((tm, d), lambda i, j: (i, 0)),
            pl.BlockSpec((None, None, 6, d), lambda i, j: (layer, geom.seq_of_row(i * tm), 0, 0)),
            pl.BlockSpec((1, d), lambda i, j: (0, 0)),
            pl.BlockSpec((d, tn), lambda i, j: (0, j)),
        ],
        out_specs=pl.BlockSpec((tm, tn), lambda i, j: (i, j)),
        scratch_shapes=[pltpu.VMEM((d // LANES, tm, LANES), F32), pltpu.VMEM((n_br, tm, d), BF16)],
        compiler_params=pltpu.CompilerParams(
            dimension_semantics=("arbitrary", "arbitrary"), vmem_limit_bytes=VMEM_LIMIT),
        name="qkv_proj",
    )(x, mod4, norm_g.reshape(1, d), w_bf16)


def _t5_bucket(rel):
    half_b = NUM_BUCKETS // 2
    max_exact = half_b // 2
    n = jnp.abs(rel)
    large = max_exact + (jnp.log(jnp.maximum(n, 1).astype(F32) / max_exact)
                         / math.log(MAX_DISTANCE / max_exact) * (half_b - max_exact)).astype(I32)
    large = jnp.minimum(large, half_b - 1)
    return jnp.where(rel > 0, half_b, 0) + jnp.where(n < max_exact, n, large)


def _attn_bias_tables(rel_bias_table):
    half = ATTN_HALF
    qi = jnp.arange(half)[:, None]
    ki = jnp.arange(3 * half)[None, :]
    off = ki - half - qi
    in_band = jnp.abs(off) <= half
    per_branch = []
    for g, (window, dilation) in enumerate(DIL_CONFIGS):
        assert window // (2 * dilation) == half
        tab = rel_bias_table[:, g * HEADS_PER_GROUP:(g + 1) * HEADS_PER_GROUP]
        b = jnp.transpose(tab[_t5_bucket(off * dilation)], (2, 0, 1)).astype(F32)
        variants = []
        for var in range(4):
            ok = in_band
            if var & 1:
                ok = ok & (ki >= half)
            if var & 2:
                ok = ok & (ki < 2 * half)
            variants.append(jnp.where(ok[None], b, NEG))
        per_branch.append(jnp.stack(variants, axis=1))
    return jnp.stack(per_branch, axis=1)


def _attn_branch(g, d, q_ref, k_ref, v_ref, bias_ref, o_ref, mx_ref, wt_ref, os_ref, ls_ref,
                 sc_ref, pr_ref, dn_ref, lt_ref, seq_len):
    half = ATTN_HALF
    scale = HEAD_DIM ** -0.5
    units = ATTN_SUPER // half
    nbc = units // d
    n_sc = seq_len // ATTN_SUPER
    last = g == len(DIL_CONFIGS) - 1

    def rows(ref, start):
        return ref[pl.ds(pl.multiple_of(start, half), half), :]

    def super_chunk(sc, carry):
        base_sc = pl.multiple_of(sc * ATTN_SUPER, ATTN_SUPER)

        def neighbours(u):
            n = u % nbc
            at_start = n == 0
            at_end = n == nbc - 1
            prev = jnp.where(at_start, base_sc - ATTN_SUPER + (u + nbc - 1) * half, base_sc + (u - 1) * half)
            nxt = jnp.where(at_end, base_sc + ATTN_SUPER + (u - nbc + 1) * half, base_sc + (u + 1) * half)
            no_prev = jnp.logical_and(at_start, sc == 0)
            no_next = jnp.logical_and(at_end, sc == n_sc - 1)
            prev = jnp.where(no_prev, base_sc + u * half, prev)
            nxt = jnp.where(no_next, base_sc + u * half, nxt)
            return prev, nxt, no_prev.astype(I32) + 2 * no_next.astype(I32)

        def window(ref, u):
            prev, nxt, var = neighbours(u)
            return jnp.concatenate([rows(ref, prev), rows(ref, base_sc + u * half), rows(ref, nxt)], axis=0), var

        def score_unit(u, c2):
            kk, var = window(k_ref, u)
            q = rows(q_ref, base_sc + u * half)
            s = lax.dot_general(q, kk, NT_DIMS, preferred_element_type=F32) * scale + bias_ref[g, var]
            sc_ref[pl.ds(pl.multiple_of(u * half, half), half), :] = s
            return c2

        lax.fori_loop(0, units, score_unit, 0, unroll=ATTN_UNROLL)

        s = sc_ref[...]
        m = jnp.max(s, axis=-1, keepdims=True)
        p = jnp.exp(s - m)
        den = jnp.sum(p, axis=-1, keepdims=True)
        pr_ref[...] = p.astype(BF16)
        dn_ref[...] = jnp.broadcast_to(den, dn_ref.shape)
        lt_ref[...] = jnp.broadcast_to(m + jnp.log(den), lt_ref.shape)

        def value_unit(u, c2):
            vv, _ = window(v_ref, u)
            blk = pl.ds(pl.multiple_of(u * half, half), half)
            o = _dot(pr_ref[blk, :], vv) / dn_ref[blk, :]
            nat = pl.ds((u % nbc) * (half * d) + u // nbc, half, stride=d)
            os_ref[nat, :] = o
            ls_ref[nat, :] = lt_ref[blk, :]
            return c2

        lax.fori_loop(0, units, value_unit, 0, unroll=ATTN_UNROLL)

        span = pl.ds(base_sc, ATTN_SUPER)
        og, lg = os_ref[...], ls_ref[...]
        if g == 0:
            o_ref[span, :] = og
            mx_ref[span, :] = lg
            wt_ref[span, :] = jnp.ones_like(lg)
        else:
            m_old = mx_ref[span, :]
            m_new = jnp.maximum(m_old, lg)
            a = jnp.exp(m_old - m_new)
            b = jnp.exp(lg - m_new)
            acc = o_ref[span, :] * a + og * b
            wt = wt_ref[span, :] * a + b
            if last:
                o_ref[span, :] = acc / wt
            else:
                o_ref[span, :] = acc
                mx_ref[span, :] = m_new
                wt_ref[span, :] = wt
        return carry

    lax.fori_loop(0, seq_len // ATTN_SUPER, super_chunk, 0)


def _attn_kernel(q_ref, k_ref, v_ref, bias_ref, *rest, seq_len):
    n_scratch = 8
    o_ref = rest[-n_scratch - 1]
    for g, (_, d) in enumerate(DIL_CONFIGS):
        @pl.when(pl.program_id(2) == g)
        def _(g=g, d=d):
            _attn_branch(g, d, q_ref, k_ref, v_ref, bias_ref, o_ref, *rest[-n_scratch:], seq_len)


def _attention_group(qkv, bias, prev_out, *, n_seq, seq_len, row_block0, t_total):
    assert seq_len % ATTN_SUPER == 0
    for _, d in DIL_CONFIGS:
        assert ATTN_SUPER % (ATTN_HALF * d) == 0
    h = HEADS_PER_GROUP
    width = h * HEAD_DIM
    n_br = len(DIL_CONFIGS)

    def col_spec(c):
        return pl.BlockSpec((seq_len, HEAD_DIM), lambda b, hh, g: (row_block0 + b, (g * 3 + c) * h + hh))

    in_specs = [col_spec(c) for c in range(3)]
    in_specs.append(pl.BlockSpec((None, n_br, 4, ATTN_HALF, 3 * ATTN_HALF),
                                 lambda b, hh, g: (hh, 0, 0, 0, 0)))
    args = [qkv] * 3 + [bias]
    aliases = {}
    if prev_out is not None:
        in_specs.append(pl.BlockSpec(memory_space=pl.ANY))
        args.append(prev_out)
        aliases = {len(args) - 1: 0}
    return pl.pallas_call(
        functools.partial(_attn_kernel, seq_len=seq_len),
        out_shape=jax.ShapeDtypeStruct((t_total, width), F32),
        grid=(n_seq, h, n_br),
        in_specs=in_specs,
        out_specs=pl.BlockSpec((seq_len, HEAD_DIM), lambda b, hh, g: (row_block0 + b, hh)),
        scratch_shapes=[pltpu.VMEM((seq_len, HEAD_DIM), F32),
                        pltpu.VMEM((seq_len, HEAD_DIM), F32),
                        pltpu.VMEM((ATTN_SUPER, HEAD_DIM), F32),
                        pltpu.VMEM((ATTN_SUPER, HEAD_DIM), F32),
                        pltpu.VMEM((ATTN_SUPER, 3 * ATTN_HALF), F32),
                        pltpu.VMEM((ATTN_SUPER, 3 * ATTN_HALF), BF16),
                        pltpu.VMEM((ATTN_SUPER, HEAD_DIM), F32),
                        pltpu.VMEM((ATTN_SUPER, HEAD_DIM), F32)],
        input_output_aliases=aliases,
        compiler_params=pltpu.CompilerParams(
            dimension_semantics=("arbitrary", "arbitrary", "arbitrary"), vmem_limit_bytes=VMEM_LIMIT),
        name="dilated_attention",
    )(*args)


def _route(h2, wrh_ref, wrl_ref, rb_ref, utri_ref, run_ref, te_ref, gt_ref, rk_ref, cnt_ref):
    n_exp = wrh_ref.shape[0]
    tm = h2.shape[0]
    per_group = n_exp // N_EXPERT_GROUPS
    h_hi, h_lo = _split_bf16(h2)
    wrh = wrh_ref[...]
    logits = (lax.dot_general(wrh, h_hi, NT_DIMS, preferred_element_type=F32)
              + lax.dot_general(wrh, h_lo, NT_DIMS, preferred_element_type=F32)
              + lax.dot_general(wrl_ref[...], h_hi, NT_DIMS, preferred_element_type=F32))
    scores = jax.nn.sigmoid(logits)
    biased = scores + rb_ref[...]

    gi = lax.broadcasted_iota(I32, (per_group, tm), 0).astype(F32)
    group_vals, group_scores = [], []
    for g in range(N_EXPERT_GROUPS):
        v = biased[g * per_group:(g + 1) * per_group, :]
        m1 = jnp.max(v, axis=0, keepdims=True)
        i1 = jnp.min(jnp.where(v == m1, gi, float(per_group)), axis=0, keepdims=True)
        m2 = jnp.max(jnp.where(gi == i1, -jnp.inf, v), axis=0, keepdims=True)
        group_vals.append(v)
        group_scores.append(m1 + m2)
    gs = jnp.concatenate(group_scores, axis=0)
    g_iota = lax.broadcasted_iota(I32, gs.shape, 0)
    rank = jnp.zeros(gs.shape, I32)
    for gp in range(N_EXPERT_GROUPS):
        row = gs[gp:gp + 1, :]
        beats = jnp.where(row > gs, 1, jnp.where(row == gs, jnp.where(g_iota > gp, 1, 0), 0))
        rank = rank + beats
    keep = rank < TOPK_GROUPS
    cur = jnp.concatenate(
        [jnp.where(keep[g:g + 1, :], group_vals[g], NEG) for g in range(N_EXPERT_GROUPS)], axis=0)

    e_iota = lax.broadcasted_iota(I32, (n_exp, tm), 0).astype(F32)
    picks, gates = [], []
    for _ in range(TOP_K):
        mk = jnp.max(cur, axis=0, keepdims=True)
        ik = jnp.min(jnp.where(cur == mk, e_iota, float(n_exp)), axis=0, keepdims=True)
        sel = e_iota == ik
        gates.append(jnp.sum(jnp.where(sel, scores, 0.0), axis=0, keepdims=True))
        cur = jnp.where(sel, -jnp.inf, cur)
        picks.append(ik)
    gt = jnp.concatenate(gates, axis=0)
    gt = gt / jnp.sum(gt, axis=0, keepdims=True) * ROUTED_SCALE
    te_ref[...] = jnp.concatenate(picks, axis=0).astype(I32)
    gt_ref[...] = gt

    hot = jnp.zeros((n_exp, tm), F32)
    for k in range(TOP_K):
        hot = hot + jnp.where(e_iota == picks[k], 1.0, 0.0)
    before = _dot(hot.astype(BF16), utri_ref[...]) + run_ref[...]
    ranks = [jnp.sum(jnp.where(e_iota == picks[k], before, 0.0), axis=0, keepdims=True)
             for k in range(TOP_K)]
    rk_ref[...] = jnp.concatenate(ranks, axis=0).astype(I32)
    run_ref[...] = run_ref[...] + jnp.sum(hot, axis=1, keepdims=True)
    cnt_ref[...] = run_ref[...]


def _post_common(mix, x_ref, mod_ref, n2_ref, wrh_ref, wrl_ref, rb_ref, utri_ref,
                 x1_ref, h2_ref, te_ref, gt_ref, rk_ref, cnt_ref, run_ref):
    @pl.when(pl.program_id(0) == 0)
    def _():
        run_ref[...] = jnp.zeros(run_ref.shape, F32)

    x1 = x_ref[...] + mod_ref[2:3, :] * mix
    x1_ref[...] = x1
    h2 = _norm_mod(x1, n2_ref[...], mod_ref[4:5, :], mod_ref[3:4, :])
    h2_ref[...] = _pack_bf16_pair(h2)
    _route(h2, wrh_ref, wrl_ref, rb_ref, utri_ref, run_ref, te_ref, gt_ref, rk_ref, cnt_ref)


def _post_attn_kernel(a_ref, wo_ref, *rest):
    mix = _dot(a_ref[...].astype(BF16), wo_ref[...])
    _post_common(mix, *rest)


def _post_pool_kernel(u_ref, up_ref, un_ref, wg_ref, cs_ref, wo_ref, *rest, geom):
    ext_ref = rest[-1]
    rest = rest[:-1]
    tm, d = u_ref.shape
    halo = POOL_HALO
    row0 = pl.program_id(0) * tm
    pos0, slen = geom.pos_and_len(row0)
    u = u_ref[...]
    ext_ref[pl.ds(halo, tm), :] = u
    ext_ref[pl.ds(0, halo), :] = jnp.where(pos0 > 0, up_ref[...], 0.0)
    ext_ref[pl.ds(halo + tm, halo), :] = jnp.where(pos0 + tm < slen, un_ref[...], 0.0)
    pos = pos0 + lax.broadcasted_iota(I32, (tm, 1), 0)
    gc = d // len(POOL_WINDOWS)
    zs = []
    for g, w in enumerate(POOL_WINDOWS):
        hw = w // 2
        assert hw <= halo
        cols = slice(g * gc, (g + 1) * gc)
        acc = ext_ref[pl.ds(halo - hw, tm), cols]
        for o in range(-hw + 1, hw):
            acc = acc + ext_ref[pl.ds(halo + o, tm), cols]
        cnt = (jnp.minimum(pos + hw, slen) - jnp.maximum(pos - hw, 0)).astype(F32)
        p = acc / cnt - u[:, cols]
        zs.append(_dot(p.astype(BF16), wg_ref[g]) * cs_ref[:, cols])
    z = jnp.concatenate(zs, axis=1)
    mix = _dot(z.astype(BF16), wo_ref[...])
    _post_common(mix, *rest)


def _post_mixer(geom, layer, mixer_args, mixer_specs, body, x, mod4, norm2_g, w_router, router_bias,
                extra_scratch=()):
    t, d = x.shape
    n_exp = w_router.shape[1]
    tm = ROW_TILE
    wr_t = w_router.T
    wr_hi = wr_t.astype(BF16)
    wr_lo = (wr_t - wr_hi.astype(F32)).astype(BF16)
    utri = (jnp.arange(tm)[:, None] < jnp.arange(tm)[None, :]).astype(BF16)
    row = lambda i: (i, 0)
    col = lambda i: (0, i)
    const = lambda i: (0, 0)
    in_specs = list(mixer_specs) + [
        pl.BlockSpec((tm, d), row),
        pl.BlockSpec((None, None, 6, d), lambda i: (layer, geom.seq_of_row(i * tm), 0, 0)),
        pl.BlockSpec((1, d), const),
        pl.BlockSpec((n_exp, d), const),
        pl.BlockSpec((n_exp, d), const),
        pl.BlockSpec((n_exp, 1), const),
        pl.BlockSpec((tm, tm), const),
    ]
    out_shape = [
        jax.ShapeDtypeStruct((t, d), F32),
        jax.ShapeDtypeStruct((t, d // 2), U32),
        jax.ShapeDtypeStruct((TOP_K, t), I32),
        jax.ShapeDtypeStruct((TOP_K, t), F32),
        jax.ShapeDtypeStruct((TOP_K, t), I32),
        jax.ShapeDtypeStruct((n_exp, 1), F32),
    ]
    out_specs = [
        pl.BlockSpec((tm, d), row),
        pl.BlockSpec((tm, d // 2), row),
        pl.BlockSpec((TOP_K, tm), col),
        pl.BlockSpec((TOP_K, tm), col),
        pl.BlockSpec((TOP_K, tm), col),
        pl.BlockSpec((n_exp, 1), const),
    ]
    return pl.pallas_call(
        body,
        out_shape=out_shape,
        grid=(t // tm,),
        in_specs=in_specs,
        out_specs=out_specs,
        scratch_shapes=[pltpu.VMEM((n_exp, 1), F32)] + list(extra_scratch),
        compiler_params=pltpu.CompilerParams(
            dimension_semantics=("arbitrary",), vmem_limit_bytes=VMEM_LIMIT),
        name="post_mixer",
    )(*mixer_args, x, mod4, norm2_g.reshape(1, d), wr_hi, wr_lo, router_bias.reshape(n_exp, 1), utri)


def _dest_kernel(te_ref, rk_ref, ps_ref, d_ref):
    n_exp = ps_ref.shape[0]
    te = te_ref[...]
    e_iota = lax.broadcasted_iota(I32, (n_exp, te.shape[1]), 0)
    ps = ps_ref[...]
    rows = [jnp.sum(jnp.where(e_iota == te[k:k + 1, :], ps, 0.0), axis=0, keepdims=True)
            for k in range(TOP_K)]
    d_ref[...] = jnp.concatenate(rows, axis=0).astype(I32) + rk_ref[...]


def _dest_slots(te, rk, pstart):
    k, t = te.shape
    n_exp = pstart.shape[0]
    tm = ROW_TILE
    col = lambda i: (0, i)
    return pl.pallas_call(
        _dest_kernel,
        out_shape=jax.ShapeDtypeStruct((k, t), I32),
        grid=(t // tm,),
        in_specs=[pl.BlockSpec((k, tm), col), pl.BlockSpec((k, tm), col),
                  pl.BlockSpec((n_exp, 1), lambda i: (0, 0))],
        out_specs=pl.BlockSpec((k, tm), col),
        name="moe_dest",
    )(te, rk, pstart.astype(F32).reshape(n_exp, 1))


def _zero_kernel(lb_ref, o_ref):
    o_ref[...] = jnp.zeros(o_ref.shape, o_ref.dtype)


def _zero_tail_blocks(last_blk, p_rows, d):
    return pl.pallas_call(
        _zero_kernel,
        out_shape=jax.ShapeDtypeStruct((p_rows, d), U32),
        grid_spec=pltpu.PrefetchScalarGridSpec(
            num_scalar_prefetch=1,
            grid=(last_blk.shape[0],),
            in_specs=[],
            out_specs=pl.BlockSpec((EXPERT_BLOCK, d), lambda e, lb: (lb[e], 0)),
        ),
        name="moe_zero_tails",
    )(last_blk)


def _dispatch_kernel(dest_ref, h_ref, xs_in_ref, xs_ref, sem):
    del xs_in_ref
    tt = h_ref.shape[0]

    def issue(tb, carry):
        t0 = pl.multiple_of(tb * SUBLANES, SUBLANES)
        for j in range(SUBLANES):
            for k in range(TOP_K):
                pltpu.async_copy(h_ref.at[pl.ds(t0 + j, 1), :],
                                 xs_ref.at[pl.ds(dest_ref[(t0 + j) * TOP_K + k], 1), :], sem,
                                 priority=k % DMA_QUEUES)
        return carry

    lax.fori_loop(0, tt // SUBLANES, issue, 0)
    for k in range(TOP_K):
        pltpu.make_async_copy(h_ref, xs_ref.at[pl.ds(0, tt), :], sem).wait()


def _dispatch(dest, h2, xs_zeroed):
    t, d = h2.shape
    tt = MOE_TOKENS
    return pl.pallas_call(
        _dispatch_kernel,
        out_shape=jax.ShapeDtypeStruct(xs_zeroed.shape, xs_zeroed.dtype),
        grid=(t // tt,),
        in_specs=[
            pl.BlockSpec((TOP_K * tt,), lambda i: (i,), memory_space=pltpu.SMEM),
            pl.BlockSpec((tt, d), lambda i: (i, 0)),
            pl.BlockSpec(memory_space=pl.ANY),
        ],
        out_specs=pl.BlockSpec(memory_space=pl.ANY),
        scratch_shapes=[pltpu.SemaphoreType.DMA(())],
        input_output_aliases={2: 0},
        compiler_params=pltpu.CompilerParams(
            dimension_semantics=("arbitrary",), has_side_effects=True),
        name="moe_dispatch",
    )(dest, h2, xs_zeroed)


def _expert_kernel(be_ref, nu_ref, x_ref, wg_ref, wu_ref, wd_ref, o_ref, wgb_ref, wub_ref, wdb_ref):
    i = pl.program_id(0)

    @pl.when(i < nu_ref[0])
    def _():
        @pl.when(jnp.logical_or(i == 0, be_ref[i] != be_ref[jnp.maximum(i - 1, 0)]))
        def _():
            wgb_ref[...] = wg_ref[...].astype(BF16)
            wub_ref[...] = wu_ref[...].astype(BF16)
            wdb_ref[...] = wd_ref[...].astype(BF16)

        half = x_ref.shape[1]
        for c in range(EXPERT_BLOCK // EXPERT_CHUNK):
            rows = pl.ds(c * EXPERT_CHUNK, EXPERT_CHUNK)
            lo, hi = _unpack_bf16_pair(x_ref[rows, :])
            lo, hi = lo.astype(BF16), hi.astype(BF16)

            def proj(w_ref, lo=lo, hi=hi):
                return _dot(lo, w_ref[pl.ds(0, half), :]) + _dot(hi, w_ref[pl.ds(half, half), :])

            hid = _silu(proj(wgb_ref)) * proj(wub_ref)
            o_ref[rows, :] = _pack_bf16_pair(_dot(hid.astype(BF16), wdb_ref[...]))


def _experts(layer, blk_expert, n_used, xs, wg, wu, wd):
    p_rows, dp = xs.shape
    d = 2 * dp
    de = wg.shape[3]
    blk = lambda i, be, nu: (jnp.minimum(i, nu[0] - 1), 0)
    wsel = lambda i, be, nu: (layer, be[i], 0, 0)
    return pl.pallas_call(
        _expert_kernel,
        out_shape=jax.ShapeDtypeStruct((p_rows, dp), U32),
        grid_spec=pltpu.PrefetchScalarGridSpec(
            num_scalar_prefetch=2,
            grid=(p_rows // EXPERT_BLOCK,),
            in_specs=[
                pl.BlockSpec((EXPERT_BLOCK, dp), blk),
                pl.BlockSpec((None, None, d, de), wsel),
                pl.BlockSpec((None, None, d, de), wsel),
                pl.BlockSpec((None, None, de, d), wsel),
            ],
            out_specs=pl.BlockSpec((EXPERT_BLOCK, dp), blk),
            scratch_shapes=[pltpu.VMEM((d, de), BF16), pltpu.VMEM((d, de), BF16), pltpu.VMEM((de, d), BF16)],
        ),
        compiler_params=pltpu.CompilerParams(
            dimension_semantics=("arbitrary",), vmem_limit_bytes=VMEM_LIMIT),
        name="moe_experts",
    )(blk_expert, n_used, xs, wg, wu, wd)


def _combine_kernel(dest_ref, dnext_ref, x1_ref, h_ref, gt_ref, mod_ref, wsg_ref, wsu_ref, wsd_ref, fg_ref,
                    ys_ref, *rest, final_norm, first_steps):
    buf_ref, sems = rest[-2:]
    out_refs = rest[:-2]
    tt = x1_ref.shape[0]
    half = h_ref.shape[1]
    i = pl.program_id(0)
    slot = i % 2

    def issue(d_ref, s):
        def body(tb, carry):
            t0 = pl.multiple_of(tb * SUBLANES, SUBLANES)
            for j in range(SUBLANES):
                for k in range(TOP_K):
                    pltpu.async_copy(ys_ref.at[pl.ds(d_ref[(t0 + j) * TOP_K + k], 1), :],
                                     buf_ref.at[s, k, pl.ds(t0 + j, 1), :], sems.at[s],
                                     priority=k % DMA_QUEUES)
            return carry

        lax.fori_loop(0, tt // SUBLANES, body, 0)

    @pl.when(i == 0)
    def _():
        issue(dest_ref, 0)

    for s in range(2):
        @pl.when(jnp.logical_and(i + 1 < pl.num_programs(0), slot == 1 - s))
        def _(s=s):
            issue(dnext_ref, s)

    lo, hi = _unpack_bf16_pair(h_ref[...])
    lo, hi = lo.astype(BF16), hi.astype(BF16)

    def proj(w_ref):
        return _dot(lo, w_ref[pl.ds(0, half), :]) + _dot(hi, w_ref[pl.ds(half, half), :])

    hid = _silu(proj(wsg_ref)) * proj(wsu_ref)
    y = _dot(hid.astype(BF16), wsd_ref[...])
    for k in range(TOP_K):
        pltpu.make_async_copy(ys_ref.at[pl.ds(0, tt), :], buf_ref.at[slot, k], sems.at[slot]).wait()
    gt = gt_ref[...]
    acc_lo = jnp.zeros((tt, half), F32)
    acc_hi = jnp.zeros((tt, half), F32)
    for k in range(TOP_K):
        e_lo, e_hi = _unpack_bf16_pair(buf_ref[slot, k])
        acc_lo = acc_lo + e_lo * gt[:, k:k + 1]
        acc_hi = acc_hi + e_hi * gt[:, k:k + 1]
    y = y + jnp.concatenate([acc_lo, acc_hi], axis=1)
    x2 = x1_ref[...] + mod_ref[5:6, :] * y
    if not final_norm:
        out_refs[0][...] = x2
        return
    ms = jnp.mean(x2 * x2, axis=-1, keepdims=True)
    x2 = x2 * lax.rsqrt(ms + EPS) * fg_ref[...]

    @pl.when(i < first_steps)
    def _():
        out_refs[0][...] = x2

    @pl.when(i >= first_steps)
    def _():
        out_refs[1][...] = x2


def _combine(geom, layer, dest, x1, h2, gate_t, mod4, wsg, wsu, wsd, final_g, ys, final_norm):
    t, d = x1.shape
    de = wsg.shape[1]
    tt = MOE_TOKENS
    n_steps = t // tt
    first_steps = geom.tp // tt
    row = lambda i: (i, 0)
    const = lambda i: (0, 0)
    if final_norm:
        out_shape = [jax.ShapeDtypeStruct((geom.tp, d), F32), jax.ShapeDtypeStruct((geom.ts, d), F32)]
        out_specs = [pl.BlockSpec((tt, d), lambda i: (jnp.minimum(i, first_steps - 1), 0)),
                     pl.BlockSpec((tt, d), lambda i: (jnp.maximum(i - first_steps, 0), 0))]
    else:
        out_shape = jax.ShapeDtypeStruct((t, d), F32)
        out_specs = pl.BlockSpec((tt, d), row)
    return pl.pallas_call(
        functools.partial(_combine_kernel, final_norm=final_norm, first_steps=first_steps),
        out_shape=out_shape,
        grid=(n_steps,),
        in_specs=[
            pl.BlockSpec((TOP_K * tt,), lambda i: (i,), memory_space=pltpu.SMEM),
            pl.BlockSpec((TOP_K * tt,), lambda i: (jnp.minimum(i + 1, n_steps - 1),), memory_space=pltpu.SMEM),
            pl.BlockSpec((tt, d), row),
            pl.BlockSpec((tt, d // 2), row),
            pl.BlockSpec((tt, TOP_K), row),
            pl.BlockSpec((None, None, 6, d), lambda i: (layer, geom.seq_of_row(i * tt), 0, 0)),
            pl.BlockSpec((d, de), const),
            pl.BlockSpec((d, de), const),
            pl.BlockSpec((de, d), const),
            pl.BlockSpec((1, d), const),
            pl.BlockSpec(memory_space=pl.ANY),
        ],
        out_specs=out_specs,
        scratch_shapes=[pltpu.VMEM((2, TOP_K, tt, d // 2), U32), pltpu.SemaphoreType.DMA((2,))],
        compiler_params=pltpu.CompilerParams(
            dimension_semantics=("arbitrary",), vmem_limit_bytes=VMEM_LIMIT),
        name="moe_combine",
    )(dest, dest, x1, h2, gate_t, mod4, wsg, wsu, wsd, final_g.reshape(1, d), ys)


def _moe(geom, layer, x1, h2, te, gt, rk, cnt, mod4, wg, wu, wd, wsg, wsu, wsd, final_g, final_norm):
    t, d = x1.shape
    n_exp = wg.shape[1]
    p_rows = t * TOP_K + n_exp * EXPERT_BLOCK
    n_blk = p_rows // EXPERT_BLOCK
    counts = cnt[:, 0].astype(I32)
    pcounts = (counts + EXPERT_BLOCK - 1) // EXPERT_BLOCK * EXPERT_BLOCK
    pend = jnp.cumsum(pcounts)
    pstart = pend - pcounts
    n_used = pend[-1] // EXPERT_BLOCK
    blk_ids = jnp.arange(n_blk, dtype=I32)
    be = jnp.sum((pend[None, :] <= (blk_ids * EXPERT_BLOCK)[:, None]).astype(I32), axis=1)
    be = jnp.minimum(be, jnp.minimum(be[n_used - 1], n_exp - 1))
    last_blk = jnp.maximum(pend // EXPERT_BLOCK - 1, 0).astype(I32)

    dest = _dest_slots(te, rk, pstart).T.reshape(-1)
    xs = _dispatch(dest, h2, _zero_tail_blocks(last_blk, p_rows, h2.shape[1]))
    ys = _experts(layer, be, n_used.reshape(1).astype(I32), xs, wg, wu, wd)
    return _combine(geom, layer, dest, x1, h2, gt.T, mod4, wsg, wsu, wsd, final_g, ys, final_norm)


def kernel(x_prompt, x_sample, c_prompt, c_sample, w_mod, b_mod, norm1_g, norm2_g, rel_bias_table, w_in_a, w_out_a, w_in_b, w_grp_b, pool_scale_b, w_out_b, w_router, router_bias, w_gate_e, w_up_e, w_down_e, w_gate_s, w_up_s, w_down_s, final_norm_g):
    bp, sp, d = x_prompt.shape
    bs, ss, _ = x_sample.shape
    geom = _Geom(bp, sp, bs, ss)
    depth = w_mod.shape[0]
    tm = ROW_TILE
    assert sp % ATTN_SUPER == 0 and ss % ATTN_SUPER == 0 and geom.tp % ss == 0
    assert sp % tm == 0 and ss % tm == 0 and tm % MOE_TOKENS == 0

    x = jnp.concatenate([x_prompt.reshape(geom.tp, d), x_sample.reshape(geom.ts, d)], axis=0)
    c = jnp.concatenate([c_prompt, c_sample], axis=0)
    mod4 = _modulation(c, w_mod, b_mod).reshape(depth, geom.nb, 6, d)
    bias = _attn_bias_tables(rel_bias_table)

    for i in range(depth):
        j = i // 2
        if i % 2 == 0:
            qkv = _qkv_proj(geom, x, mod4, i, norm1_g[i], w_in_a[j].astype(BF16), 1536)
            attn = _attention_group(qkv, bias, None, n_seq=bp, seq_len=sp, row_block0=0, t_total=geom.t)
            attn = _attention_group(qkv, bias, attn, n_seq=bs, seq_len=ss, row_block0=geom.tp // ss,
                                    t_total=geom.t)
            mixer_args = (attn, w_out_a[j].astype(BF16))
            mixer_specs = (pl.BlockSpec((tm, d), lambda r: (r, 0)),
                           pl.BlockSpec((d, d), lambda r: (0, 0)))
            body, extra = _post_attn_kernel, ()
        else:
            u = _norm_proj(geom, x, mod4, i, norm1_g[i], w_in_b[j].astype(BF16), 1024)
            hb = tm // POOL_HALO
            n_hb = geom.t // POOL_HALO
            mixer_args = (u, u, u, w_grp_b[j].astype(BF16), pool_scale_b[j].reshape(1, d),
                          w_out_b[j].astype(BF16))
            gc = d // len(POOL_WINDOWS)
            mixer_specs = (pl.BlockSpec((tm, d), lambda r: (r, 0)),
                           pl.BlockSpec((POOL_HALO, d), lambda r: (jnp.maximum(r * hb - 1, 0), 0)),
                           pl.BlockSpec((POOL_HALO, d), lambda r: (jnp.minimum((r + 1) * hb, n_hb - 1), 0)),
                           pl.BlockSpec((len(POOL_WINDOWS), gc, gc), lambda r: (0, 0, 0)),
                           pl.BlockSpec((1, d), lambda r: (0, 0)),
                           pl.BlockSpec((d, d), lambda r: (0, 0)))
            body = functools.partial(_post_pool_kernel, geom=geom)
            extra = (pltpu.VMEM((tm + 2 * POOL_HALO, d), F32),)
        x1, h2, te, gt, rk, cnt = _post_mixer(geom, i, mixer_args, mixer_specs, body, x, mod4, norm2_g[i],
                                              w_router[i], router_bias[i], extra)
        x = _moe(geom, i, x1, h2, te, gt, rk, cnt, mod4,
                 w_gate_e, w_up_e, w_down_e,
                 w_gate_s[i].astype(BF16), w_up_s[i].astype(BF16), w_down_s[i].astype(BF16),
                 final_norm_g, final_norm=(i == depth - 1))

    y_prompt, y_sample = x
    return (y_prompt.reshape(bp, sp, d), y_sample.reshape(bs, ss, d))
```

```python
import functools
import math

import jax
import jax.numpy as jnp
from jax import lax
from jax.experimental import pallas as pl
from jax.experimental.pallas import tpu as pltpu

F32 = jnp.float32
BF16 = jnp.bfloat16
I32 = jnp.int32

EPS = 1e-6
NEG = -1e30

DIL_CONFIGS = ((128, 1), (512, 4), (2048, 16))
HEADS_PER_GROUP = 8
HEAD_DIM = 128
NUM_BUCKETS = 32
MAX_DISTANCE = 1024
POOL_WINDOWS = (2, 4, 8, 16)
TOP_K = 8
N_EXPERT_GROUPS = 8
TOPK_GROUPS = 4
ROUTED_SCALE = 2.5

LANES = 128
SUBLANES = 8
DMA_QUEUES = 2
ROW_TILE = 512
EXPERT_BLOCK = 512
EXPERT_CHUNK = 256
MOE_TOKENS = 256
ATTN_SUPER = 1024
ATTN_HALF = 64
ATTN_UNROLL = 8
POOL_HALO = 8
VMEM_LIMIT = 56 * 1024 * 1024

NT_DIMS = (((1,), (1,)), ((), ()))


def _dot(a, b):
    return jnp.dot(a, b, preferred_element_type=F32)


def _split_bf16(a):
    hi = a.astype(BF16)
    lo = (a - hi.astype(F32)).astype(BF16)
    return hi, lo


def _norm_mod(x, g, sc, sh):
    ms = jnp.mean(x * x, axis=-1, keepdims=True)
    y = x * lax.rsqrt(ms + EPS) * g
    return y * (1.0 + sc) + sh


def _silu(x):
    return x * jax.nn.sigmoid(x)


U32 = jnp.uint32
_HI16 = 0xFFFF0000


def _pack_bf16_pair(x):
    n = x.shape[1] // 2
    r = lax.bitcast_convert_type(x.astype(BF16).astype(F32), U32)
    return lax.shift_right_logical(r[:, :n], jnp.uint32(16)) | (r[:, n:] & jnp.uint32(_HI16))


def _store_row_major(ref, row0, packed):
    n, w = packed.shape
    pieces = w // LANES
    for p in range(pieces):
        ref[pl.ds(row0 * pieces + p, n, stride=pieces), :] = packed[:, p * LANES:(p + 1) * LANES]


def _load_row_major(ref, row0, n, pieces):
    return jnp.concatenate([ref[pl.ds(row0 * pieces + p, n, stride=pieces), :] for p in range(pieces)], axis=1)


def _unpack_bf16_pair(p):
    lo = lax.bitcast_convert_type(lax.shift_left(p, jnp.uint32(16)), F32)
    hi = lax.bitcast_convert_type(p & jnp.uint32(_HI16), F32)
    return lo, hi


class _Geom:
    def __init__(self, bp, sp, bs, ss):
        self.bp, self.sp, self.bs, self.ss = bp, sp, bs, ss
        self.tp, self.ts = bp * sp, bs * ss
        self.t = self.tp + self.ts
        self.nb = bp + bs

    def seq_of_row(self, row):
        return jnp.where(row < self.tp, row // self.sp, self.bp + (row - self.tp) // self.ss)

    def pos_and_len(self, row):
        in_p = row < self.tp
        pos = jnp.where(in_p, row % self.sp, (row - self.tp) % self.ss)
        return pos, jnp.where(in_p, self.sp, self.ss)


def _mod_kernel(c_ref, w_ref, b_ref, o_ref):
    a_hi, a_lo = _split_bf16(_silu(c_ref[...]))
    w_hi, w_lo = _split_bf16(w_ref[...])
    o_ref[...] = _dot(a_hi, w_hi) + _dot(a_hi, w_lo) + _dot(a_lo, w_hi) + b_ref[...]


def _modulation(c, w_mod, b_mod):
    depth, d, n = w_mod.shape
    nb = c.shape[0]
    tn = 1024
    return pl.pallas_call(
        _mod_kernel,
        out_shape=jax.ShapeDtypeStruct((depth, nb, n), F32),
        grid=(depth, n // tn),
        in_specs=[
            pl.BlockSpec((nb, d), lambda l, j: (0, 0)),
            pl.BlockSpec((None, d, tn), lambda l, j: (l, 0, j)),
            pl.BlockSpec((None, 1, tn), lambda l, j: (l, 0, j)),
        ],
        out_specs=pl.BlockSpec((None, nb, tn), lambda l, j: (l, 0, j)),
        compiler_params=pltpu.CompilerParams(vmem_limit_bytes=VMEM_LIMIT),
        name="modulation",
    )(c, w_mod, b_mod.reshape(depth, 1, n))


def _proj_kernel(x_ref, mod_ref, g_ref, w_ref, o_ref, h_ref):
    @pl.when(pl.program_id(1) == 0)
    def _():
        h = _norm_mod(x_ref[...], g_ref[...], mod_ref[1:2, :], mod_ref[0:1, :])
        h_ref[...] = h.astype(BF16)

    o_ref[...] = _dot(h_ref[...], w_ref[...])


def _norm_proj(geom, x, mod4, layer, norm_g, w_bf16, tn):
    t, d = x.shape
    n = w_bf16.shape[1]
    tm = ROW_TILE
    return pl.pallas_call(
        _proj_kernel,
        out_shape=jax.ShapeDtypeStruct((t, n), F32),
        grid=(t // tm, n // tn),
        in_specs=[
            pl.BlockSpec((tm, d), lambda i, j: (i, 0)),
            pl.BlockSpec((None, None, 6, d), lambda i, j: (layer, geom.seq_of_row(i * tm), 0, 0)),
            pl.BlockSpec((1, d), lambda i, j: (0, 0)),
            pl.BlockSpec((d, tn), lambda i, j: (0, j)),
        ],
        out_specs=pl.BlockSpec((tm, tn), lambda i, j: (i, j)),
        scratch_shapes=[pltpu.VMEM((tm, d), BF16)],
        compiler_params=pltpu.CompilerParams(
            dimension_semantics=("arbitrary", "arbitrary"), vmem_limit_bytes=VMEM_LIMIT),
        name="norm_proj",
    )(x, mod4, norm_g.reshape(1, d), w_bf16)


def _qkv_kernel(x_ref, mod_ref, g_ref, w_ref, o_ref, hs_ref, hv_ref, *, cols_per_branch):
    tm = x_ref.shape[0]
    j = pl.program_id(1)

    @pl.when(j == 0)
    def _():
        h = _norm_mod(x_ref[...], g_ref[...], mod_ref[1:2, :], mod_ref[0:1, :])
        n_lane_blocks = hs_ref.shape[0]
        lanes = hs_ref.shape[2]
        for c in range(n_lane_blocks):
            hs_ref[c] = h[:, c * lanes:(c + 1) * lanes]
        for g, (_, d) in enumerate(DIL_CONFIGS):
            if d == 1:
                hv_ref[g] = h.astype(BF16)
                continue
            per_class = tm // d
            for r in range(d):
                for c in range(n_lane_blocks):
                    hv_ref[g, pl.ds(r * per_class, per_class), c * lanes:(c + 1) * lanes] = (
                        hs_ref[c, pl.ds(r, per_class, stride=d), :].astype(BF16))

    tn = w_ref.shape[1]
    o_ref[...] = _dot(hv_ref[j // (cols_per_branch // tn)], w_ref[...]).astype(BF16)


def _qkv_proj(geom, x, mod4, layer, norm_g, w_bf16, tn):
    t, d = x.shape
    n = w_bf16.shape[1]
    tm = ATTN_SUPER
    n_br = len(DIL_CONFIGS)
    cols_per_branch = n // n_br
    assert cols_per_branch % tn == 0
    return pl.pallas_call(
        functools.partial(_qkv_kernel, cols_per_branch=cols_per_branch),
        out_shape=jax.ShapeDtypeStruct((t, n), BF16),
        grid=(t // tm, n // tn),
        in_specs=[
            pl.BlockSpec((tm, d), lambda i, j: (i, 0)),
            pl.BlockSpec((None, None, 6, d), lambda i, j: (layer, geom.seq_of_row(i * tm), 0, 0)),
            pl.BlockSpec((1, d), lambda i, j: (0, 0)),
            pl.BlockSpec((d, tn), lambda i, j: (0, j)),
        ],
        out_specs=pl.BlockSpec((tm, tn), lambda i, j: (i, j)),
        scratch_shapes=[pltpu.VMEM((d // LANES, tm, LANES), F32), pltpu.VMEM((n_br, tm, d), BF16)],
        compiler_params=pltpu.CompilerParams(
            dimension_semantics=("arbitrary", "arbitrary"), vmem_limit_bytes=VMEM_LIMIT),
        name="qkv_proj",
    )(x, mod4, norm_g.reshape(1, d), w_bf16)


def _t5_bucket(rel):
    half_b = NUM_BUCKETS // 2
    max_exact = half_b // 2
    n = jnp.abs(rel)
    large = max_exact + (jnp.log(jnp.maximum(n, 1).astype(F32) / max_exact)
                         / math.log(MAX_DISTANCE / max_exact) * (half_b - max_exact)).astype(I32)
    large = jnp.minimum(large, half_b - 1)
    return jnp.where(rel > 0, half_b, 0) + jnp.where(n < max_exact, n, large)


def _attn_bias_tables(rel_bias_table):
    half = ATTN_HALF
    qi = jnp.arange(half)[:, None]
    ki = jnp.arange(3 * half)[None, :]
    off = ki - half - qi
    in_band = jnp.abs(off) <= half
    per_branch = []
    for g, (window, dilation) in enumerate(DIL_CONFIGS):
        assert window // (2 * dilation) == half
        tab = rel_bias_table[:, g * HEADS_PER_GROUP:(g + 1) * HEADS_PER_GROUP]
        b = jnp.transpose(tab[_t5_bucket(off * dilation)], (2, 0, 1)).astype(F32)
        variants = []
        for var in range(4):
            ok = in_band
            if var & 1:
                ok = ok & (ki >= half)
            if var & 2:
                ok = ok & (ki < 2 * half)
            variants.append(jnp.where(ok[None], b, NEG))
        per_branch.append(jnp.stack(variants, axis=1))
    return jnp.stack(per_branch, axis=1)


def _attn_branch(g, d, q_ref, k_ref, v_ref, bias_ref, o_ref, mx_ref, wt_ref, os_ref, ls_ref,
                 sc_ref, pr_ref, dn_ref, lt_ref, seq_len):
    half = ATTN_HALF
    scale = HEAD_DIM ** -0.5
    units = ATTN_SUPER // half
    nbc = units // d
    n_sc = seq_len // ATTN_SUPER
    last = g == len(DIL_CONFIGS) - 1

    def rows(ref, start):
        return ref[pl.ds(pl.multiple_of(start, half), half), :]

    def super_chunk(sc, carry):
        base_sc = pl.multiple_of(sc * ATTN_SUPER, ATTN_SUPER)

        def neighbours(u):
            n = u % nbc
            at_start = n == 0
            at_end = n == nbc - 1
            prev = jnp.where(at_start, base_sc - ATTN_SUPER + (u + nbc - 1) * half, base_sc + (u - 1) * half)
            nxt = jnp.where(at_end, base_sc + ATTN_SUPER + (u - nbc + 1) * half, base_sc + (u + 1) * half)
            no_prev = jnp.logical_and(at_start, sc == 0)
            no_next = jnp.logical_and(at_end, sc == n_sc - 1)
            prev = jnp.where(no_prev, base_sc + u * half, prev)
            nxt = jnp.where(no_next, base_sc + u * half, nxt)
            return prev, nxt, no_prev.astype(I32) + 2 * no_next.astype(I32)

        def window(ref, u):
            prev, nxt, var = neighbours(u)
            return jnp.concatenate([rows(ref, prev), rows(ref, base_sc + u * half), rows(ref, nxt)], axis=0), var

        def score_unit(u, c2):
            kk, var = window(k_ref, u)
            q = rows(q_ref, base_sc + u * half)
            s = lax.dot_general(q, kk, NT_DIMS, preferred_element_type=F32) * scale + bias_ref[g, var]
            sc_ref[pl.ds(pl.multiple_of(u * half, half), half), :] = s
            return c2

        lax.fori_loop(0, units, score_unit, 0, unroll=ATTN_UNROLL)

        s = sc_ref[...]
        m = jnp.max(s, axis=-1, keepdims=True)
        p = jnp.exp(s - m)
        den = jnp.sum(p, axis=-1, keepdims=True)
        pr_ref[...] = p.astype(BF16)
        dn_ref[...] = jnp.broadcast_to(den, dn_ref.shape)
        lt_ref[...] = jnp.broadcast_to(m + jnp.log(den), lt_ref.shape)

        def value_unit(u, c2):
            vv, _ = window(v_ref, u)
            blk = pl.ds(pl.multiple_of(u * half, half), half)
            o = _dot(pr_ref[blk, :], vv) / dn_ref[blk, :]
            nat = pl.ds((u % nbc) * (half * d) + u // nbc, half, stride=d)
            os_ref[nat, :] = o
            ls_ref[nat, :] = lt_ref[blk, :]
            return c2

        lax.fori_loop(0, units, value_unit, 0, unroll=ATTN_UNROLL)

        span = pl.ds(base_sc, ATTN_SUPER)
        og, lg = os_ref[...], ls_ref[...]
        if g == 0:
            o_ref[span, :] = og
            mx_ref[span, :] = lg
            wt_ref[span, :] = jnp.ones_like(lg)
        else:
            m_old = mx_ref[span, :]
            m_new = jnp.maximum(m_old, lg)
            a = jnp.exp(m_old - m_new)
            b = jnp.exp(lg - m_new)
            acc = o_ref[span, :] * a + og * b
            wt = wt_ref[span, :] * a + b
            if last:
                o_ref[span, :] = acc / wt
            else:
                o_ref[span, :] = acc
                mx_ref[span, :] = m_new
                wt_ref[span, :] = wt
        return carry

    lax.fori_loop(0, seq_len // ATTN_SUPER, super_chunk, 0)


def _attn_kernel(q_ref, k_ref, v_ref, bias_ref, *rest, seq_len):
    n_scratch = 8
    o_ref = rest[-n_scratch - 1]
    for g, (_, d) in enumerate(DIL_CONFIGS):
        @pl.when(pl.program_id(2) == g)
        def _(g=g, d=d):
            _attn_branch(g, d, q_ref, k_ref, v_ref, bias_ref, o_ref, *rest[-n_scratch:], seq_len)


def _attention_group(qkv, bias, prev_out, *, n_seq, seq_len, row_block0, t_total):
    assert seq_len % ATTN_SUPER == 0
    for _, d in DIL_CONFIGS:
        assert ATTN_SUPER % (ATTN_HALF * d) == 0
    h = HEADS_PER_GROUP
    width = h * HEAD_DIM
    n_br = len(DIL_CONFIGS)

    def col_spec(c):
        return pl.BlockSpec((seq_len, HEAD_DIM), lambda b, hh, g: (row_block0 + b, (g * 3 + c) * h + hh))

    in_specs = [col_spec(c) for c in range(3)]
    in_specs.append(pl.BlockSpec((None, n_br, 4, ATTN_HALF, 3 * ATTN_HALF),
                                 lambda b, hh, g: (hh, 0, 0, 0, 0)))
    args = [qkv] * 3 + [bias]
    aliases = {}
    if prev_out is not None:
        in_specs.append(pl.BlockSpec(memory_space=pl.ANY))
        args.append(prev_out)
        aliases = {len(args) - 1: 0}
    return pl.pallas_call(
        functools.partial(_attn_kernel, seq_len=seq_len),
        out_shape=jax.ShapeDtypeStruct((t_total, width), F32),
        grid=(n_seq, h, n_br),
        in_specs=in_specs,
        out_specs=pl.BlockSpec((seq_len, HEAD_DIM), lambda b, hh, g: (row_block0 + b, hh)),
        scratch_shapes=[pltpu.VMEM((seq_len, HEAD_DIM), F32),
                        pltpu.VMEM((seq_len, HEAD_DIM), F32),
                        pltpu.VMEM((ATTN_SUPER, HEAD_DIM), F32),
                        pltpu.VMEM((ATTN_SUPER, HEAD_DIM), F32),
                        pltpu.VMEM((ATTN_SUPER, 3 * ATTN_HALF), F32),
                        pltpu.VMEM((ATTN_SUPER, 3 * ATTN_HALF), BF16),
                        pltpu.VMEM((ATTN_SUPER, HEAD_DIM), F32),
                        pltpu.VMEM((ATTN_SUPER, HEAD_DIM), F32)],
        input_output_aliases=aliases,
        compiler_params=pltpu.CompilerParams(
            dimension_semantics=("arbitrary", "arbitrary", "arbitrary"), vmem_limit_bytes=VMEM_LIMIT),
        name="dilated_attention",
    )(*args)


def _route(h2, wrh_ref, wrl_ref, rb_ref, utri_ref, run_ref, te_ref, gt_ref, rk_ref, cnt_ref):
    n_exp = wrh_ref.shape[0]
    tm = h2.shape[0]
    per_group = n_exp // N_EXPERT_GROUPS
    h_hi, h_lo = _split_bf16(h2)
    wrh = wrh_ref[...]
    logits = (lax.dot_general(wrh, h_hi, NT_DIMS, preferred_element_type=F32)
              + lax.dot_general(wrh, h_lo, NT_DIMS, preferred_element_type=F32)
              + lax.dot_general(wrl_ref[...], h_hi, NT_DIMS, preferred_element_type=F32))
    scores = jax.nn.sigmoid(logits)
    biased = scores + rb_ref[...]

    gi = lax.broadcasted_iota(I32, (per_group, tm), 0).astype(F32)
    group_vals, group_scores = [], []
    for g in range(N_EXPERT_GROUPS):
        v = biased[g * per_group:(g + 1) * per_group, :]
        m1 = jnp.max(v, axis=0, keepdims=True)
        i1 = jnp.min(jnp.where(v == m1, gi, float(per_group)), axis=0, keepdims=True)
        m2 = jnp.max(jnp.where(gi == i1, -jnp.inf, v), axis=0, keepdims=True)
        group_vals.append(v)
        group_scores.append(m1 + m2)
    gs = jnp.concatenate(group_scores, axis=0)
    g_iota = lax.broadcasted_iota(I32, gs.shape, 0)
    rank = jnp.zeros(gs.shape, I32)
    for gp in range(N_EXPERT_GROUPS):
        row = gs[gp:gp + 1, :]
        beats = jnp.where(row > gs, 1, jnp.where(row == gs, jnp.where(g_iota > gp, 1, 0), 0))
        rank = rank + beats
    keep = rank < TOPK_GROUPS
    cur = jnp.concatenate(
        [jnp.where(keep[g:g + 1, :], group_vals[g], NEG) for g in range(N_EXPERT_GROUPS)], axis=0)

    e_iota = lax.broadcasted_iota(I32, (n_exp, tm), 0).astype(F32)
    picks, gates = [], []
    for _ in range(TOP_K):
        mk = jnp.max(cur, axis=0, keepdims=True)
        ik = jnp.min(jnp.where(cur == mk, e_iota, float(n_exp)), axis=0, keepdims=True)
        sel = e_iota == ik
        gates.append(jnp.sum(jnp.where(sel, scores, 0.0), axis=0, keepdims=True))
        cur = jnp.where(sel, -jnp.inf, cur)
        picks.append(ik)
    gt = jnp.concatenate(gates, axis=0)
    gt = gt / jnp.sum(gt, axis=0, keepdims=True) * ROUTED_SCALE
    te_ref[...] = jnp.concatenate(picks, axis=0).astype(I32)
    gt_ref[...] = gt

    hot = jnp.zeros((n_exp, tm), F32)
    for k in range(TOP_K):
        hot = hot + jnp.where(e_iota == picks[k], 1.0, 0.0)
    before = _dot(hot.astype(BF16), utri_ref[...]) + run_ref[...]
    ranks = [jnp.sum(jnp.where(e_iota == picks[k], before, 0.0), axis=0, keepdims=True)
             for k in range(TOP_K)]
    rk_ref[...] = jnp.concatenate(ranks, axis=0).astype(I32)
    run_ref[...] = run_ref[...] + jnp.sum(hot, axis=1, keepdims=True)
    cnt_ref[...] = run_ref[...]


def _post_common(mix, x_ref, mod_ref, n2_ref, wrh_ref, wrl_ref, rb_ref, utri_ref,
                 x1_ref, h2_ref, te_ref, gt_ref, rk_ref, cnt_ref, run_ref):
    @pl.when(pl.program_id(0) == 0)
    def _():
        run_ref[...] = jnp.zeros(run_ref.shape, F32)

    x1 = x_ref[...] + mod_ref[2:3, :] * mix
    x1_ref[...] = x1
    h2 = _norm_mod(x1, n2_ref[...], mod_ref[4:5, :], mod_ref[3:4, :])
    _store_row_major(h2_ref, 0, _pack_bf16_pair(h2))
    _route(h2, wrh_ref, wrl_ref, rb_ref, utri_ref, run_ref, te_ref, gt_ref, rk_ref, cnt_ref)


def _post_attn_kernel(a_ref, wo_ref, *rest):
    mix = _dot(a_ref[...].astype(BF16), wo_ref[...])
    _post_common(mix, *rest)


def _post_pool_kernel(u_ref, up_ref, un_ref, wg_ref, cs_ref, wo_ref, *rest, geom):
    ext_ref = rest[-1]
    rest = rest[:-1]
    tm, d = u_ref.shape
    halo = POOL_HALO
    row0 = pl.program_id(0) * tm
    pos0, slen = geom.pos_and_len(row0)
    u = u_ref[...]
    ext_ref[pl.ds(halo, tm), :] = u
    ext_ref[pl.ds(0, halo), :] = jnp.where(pos0 > 0, up_ref[...], 0.0)
    ext_ref[pl.ds(halo + tm, halo), :] = jnp.where(pos0 + tm < slen, un_ref[...], 0.0)
    pos = pos0 + lax.broadcasted_iota(I32, (tm, 1), 0)
    gc = d // len(POOL_WINDOWS)
    zs = []
    for g, w in enumerate(POOL_WINDOWS):
        hw = w // 2
        assert hw <= halo
        cols = slice(g * gc, (g + 1) * gc)
        acc = ext_ref[pl.ds(halo - hw, tm), cols]
        for o in range(-hw + 1, hw):
            acc = acc + ext_ref[pl.ds(halo + o, tm), cols]
        cnt = (jnp.minimum(pos + hw, slen) - jnp.maximum(pos - hw, 0)).astype(F32)
        p = acc / cnt - u[:, cols]
        zs.append(_dot(p.astype(BF16), wg_ref[g]) * cs_ref[:, cols])
    z = jnp.concatenate(zs, axis=1)
    mix = _dot(z.astype(BF16), wo_ref[...])
    _post_common(mix, *rest)


def _post_mixer(geom, layer, mixer_args, mixer_specs, body, x, mod4, norm2_g, w_router, router_bias,
                extra_scratch=()):
    t, d = x.shape
    n_exp = w_router.shape[1]
    tm = ROW_TILE
    wr_t = w_router.T
    wr_hi = wr_t.astype(BF16)
    wr_lo = (wr_t - wr_hi.astype(F32)).astype(BF16)
    utri = (jnp.arange(tm)[:, None] < jnp.arange(tm)[None, :]).astype(BF16)
    row = lambda i: (i, 0)
    col = lambda i: (0, i)
    const = lambda i: (0, 0)
    in_specs = list(mixer_specs) + [
        pl.BlockSpec((tm, d), row),
        pl.BlockSpec((None, None, 6, d), lambda i: (layer, geom.seq_of_row(i * tm), 0, 0)),
        pl.BlockSpec((1, d), const),
        pl.BlockSpec((n_exp, d), const),
        pl.BlockSpec((n_exp, d), const),
        pl.BlockSpec((n_exp, 1), const),
        pl.BlockSpec((tm, tm), const),
    ]
    out_shape = [
        jax.ShapeDtypeStruct((t, d), F32),
        jax.ShapeDtypeStruct((t * (d // 2 // LANES), LANES), U32),
        jax.ShapeDtypeStruct((TOP_K, t), I32),
        jax.ShapeDtypeStruct((TOP_K, t), F32),
        jax.ShapeDtypeStruct((TOP_K, t), I32),
        jax.ShapeDtypeStruct((n_exp, 1), F32),
    ]
    out_specs = [
        pl.BlockSpec((tm, d), row),
        pl.BlockSpec((tm * (d // 2 // LANES), LANES), row),
        pl.BlockSpec((TOP_K, tm), col),
        pl.BlockSpec((TOP_K, tm), col),
        pl.BlockSpec((TOP_K, tm), col),
        pl.BlockSpec((n_exp, 1), const),
    ]
    return pl.pallas_call(
        body,
        out_shape=out_shape,
        grid=(t // tm,),
        in_specs=in_specs,
        out_specs=out_specs,
        scratch_shapes=[pltpu.VMEM((n_exp, 1), F32)] + list(extra_scratch),
        compiler_params=pltpu.CompilerParams(
            dimension_semantics=("arbitrary",), vmem_limit_bytes=VMEM_LIMIT),
        name="post_mixer",
    )(*mixer_args, x, mod4, norm2_g.reshape(1, d), wr_hi, wr_lo, router_bias.reshape(n_exp, 1), utri)


def _dest_kernel(te_ref, rk_ref, ps_ref, d_ref):
    n_exp = ps_ref.shape[0]
    te = te_ref[...]
    e_iota = lax.broadcasted_iota(I32, (n_exp, te.shape[1]), 0)
    ps = ps_ref[...]
    rows = [jnp.sum(jnp.where(e_iota == te[k:k + 1, :], ps, 0.0), axis=0, keepdims=True)
            for k in range(TOP_K)]
    d_ref[...] = jnp.concatenate(rows, axis=0).astype(I32) + rk_ref[...]


def _dest_slots(te, rk, pstart):
    k, t = te.shape
    n_exp = pstart.shape[0]
    tm = ROW_TILE
    col = lambda i: (0, i)
    return pl.pallas_call(
        _dest_kernel,
        out_shape=jax.ShapeDtypeStruct((k, t), I32),
        grid=(t // tm,),
        in_specs=[pl.BlockSpec((k, tm), col), pl.BlockSpec((k, tm), col),
                  pl.BlockSpec((n_exp, 1), lambda i: (0, 0))],
        out_specs=pl.BlockSpec((k, tm), col),
        name="moe_dest",
    )(te, rk, pstart.astype(F32).reshape(n_exp, 1))


def _zero_kernel(lb_ref, o_ref):
    o_ref[...] = jnp.zeros(o_ref.shape, o_ref.dtype)


def _zero_tail_blocks(last_blk, p_rows, pieces):
    return pl.pallas_call(
        _zero_kernel,
        out_shape=jax.ShapeDtypeStruct((p_rows, pieces, LANES), U32),
        grid_spec=pltpu.PrefetchScalarGridSpec(
            num_scalar_prefetch=1,
            grid=(last_blk.shape[0],),
            in_specs=[],
            out_specs=pl.BlockSpec((EXPERT_BLOCK, pieces, LANES), lambda e, lb: (lb[e], 0, 0)),
        ),
        name="moe_zero_tails",
    )(last_blk)


def _dispatch_kernel(dest_ref, h_ref, xs_in_ref, xs_ref, sem):
    del xs_in_ref
    pieces = xs_ref.shape[1]
    tt = h_ref.shape[0] // pieces

    def issue(tb, carry):
        t0 = pl.multiple_of(tb * SUBLANES, SUBLANES)
        for j in range(SUBLANES):
            for k in range(TOP_K):
                pltpu.async_copy(h_ref.at[pl.ds((t0 + j) * pieces, pieces), :],
                                 xs_ref.at[dest_ref[(t0 + j) * TOP_K + k]], sem,
                                 priority=k % DMA_QUEUES)
        return carry

    lax.fori_loop(0, tt // SUBLANES, issue, 0)
    for k in range(TOP_K):
        pltpu.make_async_copy(h_ref, h_ref, sem).wait()


def _dispatch(dest, h2, xs_zeroed):
    pieces = xs_zeroed.shape[1]
    t = h2.shape[0] // pieces
    tt = MOE_TOKENS
    return pl.pallas_call(
        _dispatch_kernel,
        out_shape=jax.ShapeDtypeStruct(xs_zeroed.shape, xs_zeroed.dtype),
        grid=(t // tt,),
        in_specs=[
            pl.BlockSpec((TOP_K * tt,), lambda i: (i,), memory_space=pltpu.SMEM),
            pl.BlockSpec((tt * pieces, LANES), lambda i: (i, 0)),
            pl.BlockSpec(memory_space=pl.ANY),
        ],
        out_specs=pl.BlockSpec(memory_space=pl.ANY),
        scratch_shapes=[pltpu.SemaphoreType.DMA(())],
        input_output_aliases={2: 0},
        compiler_params=pltpu.CompilerParams(
            dimension_semantics=("arbitrary",), has_side_effects=True),
        name="moe_dispatch",
    )(dest, h2, xs_zeroed)


def _expert_kernel(be_ref, nu_ref, x_ref, wg_ref, wu_ref, wd_ref, o_ref, wgb_ref, wub_ref, wdb_ref):
    i = pl.program_id(0)

    @pl.when(i < nu_ref[0])
    def _():
        @pl.when(jnp.logical_or(i == 0, be_ref[i] != be_ref[jnp.maximum(i - 1, 0)]))
        def _():
            wgb_ref[...] = wg_ref[...].astype(BF16)
            wub_ref[...] = wu_ref[...].astype(BF16)
            wdb_ref[...] = wd_ref[...].astype(BF16)

        half = wgb_ref.shape[0] // 2
        pieces = half // LANES
        for c in range(EXPERT_BLOCK // EXPERT_CHUNK):
            lo, hi = _unpack_bf16_pair(_load_row_major(x_ref, c * EXPERT_CHUNK, EXPERT_CHUNK, pieces))
            lo, hi = lo.astype(BF16), hi.astype(BF16)

            def proj(w_ref, lo=lo, hi=hi):
                return _dot(lo, w_ref[pl.ds(0, half), :]) + _dot(hi, w_ref[pl.ds(half, half), :])

            hid = _silu(proj(wgb_ref)) * proj(wub_ref)
            _store_row_major(o_ref, c * EXPERT_CHUNK, _pack_bf16_pair(_dot(hid.astype(BF16), wdb_ref[...])))


def _experts(layer, blk_expert, n_used, xs, wg, wu, wd):
    d, de = wg.shape[2], wg.shape[3]
    pieces = d // 2 // LANES
    p_rows = xs.shape[0] // pieces
    blk = lambda i, be, nu: (jnp.minimum(i, nu[0] - 1), 0)
    wsel = lambda i, be, nu: (layer, be[i], 0, 0)
    return pl.pallas_call(
        _expert_kernel,
        out_shape=jax.ShapeDtypeStruct(xs.shape, U32),
        grid_spec=pltpu.PrefetchScalarGridSpec(
            num_scalar_prefetch=2,
            grid=(p_rows // EXPERT_BLOCK,),
            in_specs=[
                pl.BlockSpec((EXPERT_BLOCK * pieces, LANES), blk),
                pl.BlockSpec((None, None, d, de), wsel),
                pl.BlockSpec((None, None, d, de), wsel),
                pl.BlockSpec((None, None, de, d), wsel),
            ],
            out_specs=pl.BlockSpec((EXPERT_BLOCK * pieces, LANES), blk),
            scratch_shapes=[pltpu.VMEM((d, de), BF16), pltpu.VMEM((d, de), BF16), pltpu.VMEM((de, d), BF16)],
        ),
        compiler_params=pltpu.CompilerParams(
            dimension_semantics=("arbitrary",), vmem_limit_bytes=VMEM_LIMIT),
        name="moe_experts",
    )(blk_expert, n_used, xs, wg, wu, wd)


def _combine_kernel(dest_ref, dnext_ref, x1_ref, h_ref, gt_ref, mod_ref, wsg_ref, wsu_ref, wsd_ref, fg_ref,
                    ys_ref, *rest, final_norm, first_steps):
    buf_ref, sems = rest[-2:]
    out_refs = rest[:-2]
    tt = x1_ref.shape[0]
    half = x1_ref.shape[1] // 2
    pieces = half // LANES
    i = pl.program_id(0)
    slot = i % 2

    def issue(d_ref, s):
        def body(tb, carry):
            t0 = pl.multiple_of(tb * SUBLANES, SUBLANES)
            for j in range(SUBLANES):
                for k in range(TOP_K):
                    pltpu.async_copy(ys_ref.at[d_ref[(t0 + j) * TOP_K + k]],
                                     buf_ref.at[s, k, pl.ds((t0 + j) * pieces, pieces), :], sems.at[s],
                                     priority=k % DMA_QUEUES)
            return carry

        lax.fori_loop(0, tt // SUBLANES, body, 0)

    @pl.when(i == 0)
    def _():
        issue(dest_ref, 0)

    for s in range(2):
        @pl.when(jnp.logical_and(i + 1 < pl.num_programs(0), slot == 1 - s))
        def _(s=s):
            issue(dnext_ref, s)

    lo, hi = _unpack_bf16_pair(_load_row_major(h_ref, 0, tt, pieces))
    lo, hi = lo.astype(BF16), hi.astype(BF16)

    def proj(w_ref):
        return _dot(lo, w_ref[pl.ds(0, half), :]) + _dot(hi, w_ref[pl.ds(half, half), :])

    hid = _silu(proj(wsg_ref)) * proj(wsu_ref)
    y = _dot(hid.astype(BF16), wsd_ref[...])
    for k in range(TOP_K):
        pltpu.make_async_copy(buf_ref.at[slot, k], buf_ref.at[slot, k], sems.at[slot]).wait()
    gt = gt_ref[...]
    acc_lo = jnp.zeros((tt, half), F32)
    acc_hi = jnp.zeros((tt, half), F32)
    for k in range(TOP_K):
        e_lo, e_hi = _unpack_bf16_pair(_load_row_major(buf_ref.at[slot, k], 0, tt, pieces))
        acc_lo = acc_lo + e_lo * gt[:, k:k + 1]
        acc_hi = acc_hi + e_hi * gt[:, k:k + 1]
    y = y + jnp.concatenate([acc_lo, acc_hi], axis=1)
    x2 = x1_ref[...] + mod_ref[5:6, :] * y
    if not final_norm:
        out_refs[0][...] = x2
        return
    ms = jnp.mean(x2 * x2, axis=-1, keepdims=True)
    x2 = x2 * lax.rsqrt(ms + EPS) * fg_ref[...]

    @pl.when(i < first_steps)
    def _():
        out_refs[0][...] = x2

    @pl.when(i >= first_steps)
    def _():
        out_refs[1][...] = x2


def _combine(geom, layer, dest, x1, h2, gate_t, mod4, wsg, wsu, wsd, final_g, ys, final_norm):
    t, d = x1.shape
    de = wsg.shape[1]
    tt = MOE_TOKENS
    n_steps = t // tt
    first_steps = geom.tp // tt
    row = lambda i: (i, 0)
    const = lambda i: (0, 0)
    if final_norm:
        out_shape = [jax.ShapeDtypeStruct((geom.tp, d), F32), jax.ShapeDtypeStruct((geom.ts, d), F32)]
        out_specs = [pl.BlockSpec((tt, d), lambda i: (jnp.minimum(i, first_steps - 1), 0)),
                     pl.BlockSpec((tt, d), lambda i: (jnp.maximum(i - first_steps, 0), 0))]
    else:
        out_shape = jax.ShapeDtypeStruct((t, d), F32)
        out_specs = pl.BlockSpec((tt, d), row)
    return pl.pallas_call(
        functools.partial(_combine_kernel, final_norm=final_norm, first_steps=first_steps),
        out_shape=out_shape,
        grid=(n_steps,),
        in_specs=[
            pl.BlockSpec((TOP_K * tt,), lambda i: (i,), memory_space=pltpu.SMEM),
            pl.BlockSpec((TOP_K * tt,), lambda i: (jnp.minimum(i + 1, n_steps - 1),), memory_space=pltpu.SMEM),
            pl.BlockSpec((tt, d), row),
            pl.BlockSpec((tt * (d // 2 // LANES), LANES), row),
            pl.BlockSpec((tt, TOP_K), row),
            pl.BlockSpec((None, None, 6, d), lambda i: (layer, geom.seq_of_row(i * tt), 0, 0)),
            pl.BlockSpec((d, de), const),
            pl.BlockSpec((d, de), const),
            pl.BlockSpec((de, d), const),
            pl.BlockSpec((1, d), const),
            pl.BlockSpec(memory_space=pl.ANY),
        ],
        out_specs=out_specs,
        scratch_shapes=[pltpu.VMEM((2, TOP_K, tt * (d // 2 // LANES), LANES), U32), pltpu.SemaphoreType.DMA((2,))],
        compiler_params=pltpu.CompilerParams(
            dimension_semantics=("arbitrary",), vmem_limit_bytes=VMEM_LIMIT),
        name="moe_combine",
    )(dest, dest, x1, h2, gate_t, mod4, wsg, wsu, wsd, final_g.reshape(1, d), ys)


def _moe(geom, layer, x1, h2, te, gt, rk, cnt, mod4, wg, wu, wd, wsg, wsu, wsd, final_g, final_norm):
    t, d = x1.shape
    n_exp = wg.shape[1]
    p_rows = t * TOP_K + n_exp * EXPERT_BLOCK
    n_blk = p_rows // EXPERT_BLOCK
    counts = cnt[:, 0].astype(I32)
    pcounts = (counts + EXPERT_BLOCK - 1) // EXPERT_BLOCK * EXPERT_BLOCK
    pend = jnp.cumsum(pcounts)
    pstart = pend - pcounts
    n_used = pend[-1] // EXPERT_BLOCK
    blk_ids = jnp.arange(n_blk, dtype=I32)
    be = jnp.sum((pend[None, :] <= (blk_ids * EXPERT_BLOCK)[:, None]).astype(I32), axis=1)
    be = jnp.minimum(be, jnp.minimum(be[n_used - 1], n_exp - 1))
    last_blk = jnp.maximum(pend // EXPERT_BLOCK - 1, 0).astype(I32)

    dest = _dest_slots(te, rk, pstart).T.reshape(-1)
    pieces = d // 2 // LANES
    xs = _dispatch(dest, h2, _zero_tail_blocks(last_blk, p_rows, pieces))
    ys = _experts(layer, be, n_used.reshape(1).astype(I32), xs.reshape(p_rows * pieces, LANES), wg, wu, wd)
    ys = ys.reshape(p_rows, pieces, LANES)
    return _combine(geom, layer, dest, x1, h2, gt.T, mod4, wsg, wsu, wsd, final_g, ys, final_norm)


def kernel(x_prompt, x_sample, c_prompt, c_sample, w_mod, b_mod, norm1_g, norm2_g, rel_bias_table, w_in_a, w_out_a, w_in_b, w_grp_b, pool_scale_b, w_out_b, w_router, router_bias, w_gate_e, w_up_e, w_down_e, w_gate_s, w_up_s, w_down_s, final_norm_g):
    bp, sp, d = x_prompt.shape
    bs, ss, _ = x_sample.shape
    geom = _Geom(bp, sp, bs, ss)
    depth = w_mod.shape[0]
    tm = ROW_TILE
    assert sp % ATTN_SUPER == 0 and ss % ATTN_SUPER == 0 and geom.tp % ss == 0
    assert sp % tm == 0 and ss % tm == 0 and tm % MOE_TOKENS == 0

    x = jnp.concatenate([x_prompt.reshape(geom.tp, d), x_sample.reshape(geom.ts, d)], axis=0)
    c = jnp.concatenate([c_prompt, c_sample], axis=0)
    mod4 = _modulation(c, w_mod, b_mod).reshape(depth, geom.nb, 6, d)
    bias = _attn_bias_tables(rel_bias_table)

    for i in range(depth):
        j = i // 2
        if i % 2 == 0:
            qkv = _qkv_proj(geom, x, mod4, i, norm1_g[i], w_in_a[j].astype(BF16), 1536)
            attn = _attention_group(qkv, bias, None, n_seq=bp, seq_len=sp, row_block0=0, t_total=geom.t)
            attn = _attention_group(qkv, bias, attn, n_seq=bs, seq_len=ss, row_block0=geom.tp // ss,
                                    t_total=geom.t)
            mixer_args = (attn, w_out_a[j].astype(BF16))
            mixer_specs = (pl.BlockSpec((tm, d), lambda r: (r, 0)),
                           pl.BlockSpec((d, d), lambda r: (0, 0)))
            body, extra = _post_attn_kernel, ()
        else:
            u = _norm_proj(geom, x, mod4, i, norm1_g[i], w_in_b[j].astype(BF16), 1024)
            hb = tm // POOL_HALO
            n_hb = geom.t // POOL_HALO
            mixer_args = (u, u, u, w_grp_b[j].astype(BF16), pool_scale_b[j].reshape(1, d),
                          w_out_b[j].astype(BF16))
            gc = d // len(POOL_WINDOWS)
            mixer_specs = (pl.BlockSpec((tm, d), lambda r: (r, 0)),
                           pl.BlockSpec((POOL_HALO, d), lambda r: (jnp.maximum(r * hb - 1, 0), 0)),
                           pl.BlockSpec((POOL_HALO, d), lambda r: (jnp.minimum((r + 1) * hb, n_hb - 1), 0)),
                           pl.BlockSpec((len(POOL_WINDOWS), gc, gc), lambda r: (0, 0, 0)),
                           pl.BlockSpec((1, d), lambda r: (0, 0)),
                           pl.BlockSpec((d, d), lambda r: (0, 0)))
            body = functools.partial(_post_pool_kernel, geom=geom)
            extra = (pltpu.VMEM((tm + 2 * POOL_HALO, d), F32),)
        x1, h2, te, gt, rk, cnt = _post_mixer(geom, i, mixer_args, mixer_specs, body, x, mod4, norm2_g[i],
                                              w_router[i], router_bias[i], extra)
        x = _moe(geom, i, x1, h2, te, gt, rk, cnt, mod4,
                 w_gate_e, w_up_e, w_down_e,
                 w_gate_s[i].astype(BF16), w_up_s[i].astype(BF16), w_down_s[i].astype(BF16),
                 final_norm_g, final_norm=(i == depth - 1))

    y_prompt, y_sample = x
    return (y_prompt.reshape(bp, sp, d), y_sample.reshape(bs, ss, d))
```

```python
import functools
import math

import jax
import jax.numpy as jnp
from jax import lax
from jax.experimental import pallas as pl
from jax.experimental.pallas import tpu as pltpu

F32 = jnp.float32
BF16 = jnp.bfloat16
I32 = jnp.int32

EPS = 1e-6
NEG = -1e30

DIL_CONFIGS = ((128, 1), (512, 4), (2048, 16))
HEADS_PER_GROUP = 8
HEAD_DIM = 128
NUM_BUCKETS = 32
MAX_DISTANCE = 1024
POOL_WINDOWS = (2, 4, 8, 16)
TOP_K = 8
N_EXPERT_GROUPS = 8
TOPK_GROUPS = 4
ROUTED_SCALE = 2.5

LANES = 128
SUBLANES = 8
DMA_QUEUES = 2
ROW_TILE = 512
EXPERT_BLOCK = 512
EXPERT_CHUNK = 512
MOE_TOKENS = 256
ATTN_SUPER = 1024
ATTN_HALF = 64
ATTN_UNROLL = 16
POOL_HALO = 8
VMEM_LIMIT = 56 * 1024 * 1024

NT_DIMS = (((1,), (1,)), ((), ()))


def _dot(a, b):
    return jnp.dot(a, b, preferred_element_type=F32)


def _split_bf16(a):
    hi = a.astype(BF16)
    lo = (a - hi.astype(F32)).astype(BF16)
    return hi, lo


def _norm_mod(x, g, sc, sh):
    ms = jnp.mean(x * x, axis=-1, keepdims=True)
    y = x * lax.rsqrt(ms + EPS) * g
    return y * (1.0 + sc) + sh


def _silu(x):
    return x * jax.nn.sigmoid(x)


U32 = jnp.uint32
_HI16 = 0xFFFF0000


def _pack_bf16_pair(x):
    n = x.shape[1] // 2
    r = lax.bitcast_convert_type(x.astype(BF16).astype(F32), U32)
    return lax.shift_right_logical(r[:, :n], jnp.uint32(16)) | (r[:, n:] & jnp.uint32(_HI16))


def _store_row_major(ref, row0, packed):
    n, w = packed.shape
    pieces = w // LANES
    for p in range(pieces):
        ref[pl.ds(row0 * pieces + p, n, stride=pieces), :] = packed[:, p * LANES:(p + 1) * LANES]


def _load_row_major(ref, row0, n, pieces):
    return jnp.concatenate([ref[pl.ds(row0 * pieces + p, n, stride=pieces), :] for p in range(pieces)], axis=1)


def _unpack_bf16_pair(p):
    lo = lax.bitcast_convert_type(lax.shift_left(p, jnp.uint32(16)), F32)
    hi = lax.bitcast_convert_type(p & jnp.uint32(_HI16), F32)
    return lo, hi


class _Geom:
    def __init__(self, bp, sp, bs, ss):
        self.bp, self.sp, self.bs, self.ss = bp, sp, bs, ss
        self.tp, self.ts = bp * sp, bs * ss
        self.t = self.tp + self.ts
        self.nb = bp + bs

    def seq_of_row(self, row):
        return jnp.where(row < self.tp, row // self.sp, self.bp + (row - self.tp) // self.ss)

    def pos_and_len(self, row):
        in_p = row < self.tp
        pos = jnp.where(in_p, row % self.sp, (row - self.tp) % self.ss)
        return pos, jnp.where(in_p, self.sp, self.ss)


def _mod_kernel(c_ref, w_ref, b_ref, o_ref):
    a_hi, a_lo = _split_bf16(_silu(c_ref[...]))
    w_hi, w_lo = _split_bf16(w_ref[...])
    o_ref[...] = _dot(a_hi, w_hi) + _dot(a_hi, w_lo) + _dot(a_lo, w_hi) + b_ref[...]


def _modulation(c, w_mod, b_mod):
    depth, d, n = w_mod.shape
    nb = c.shape[0]
    tn = 1024
    return pl.pallas_call(
        _mod_kernel,
        out_shape=jax.ShapeDtypeStruct((depth, nb, n), F32),
        grid=(depth, n // tn),
        in_specs=[
            pl.BlockSpec((nb, d), lambda l, j: (0, 0)),
            pl.BlockSpec((None, d, tn), lambda l, j: (l, 0, j)),
            pl.BlockSpec((None, 1, tn), lambda l, j: (l, 0, j)),
        ],
        out_specs=pl.BlockSpec((None, nb, tn), lambda l, j: (l, 0, j)),
        compiler_params=pltpu.CompilerParams(vmem_limit_bytes=VMEM_LIMIT),
        name="modulation",
    )(c, w_mod, b_mod.reshape(depth, 1, n))


def _proj_kernel(x_ref, mod_ref, g_ref, w_ref, o_ref, h_ref):
    @pl.when(pl.program_id(1) == 0)
    def _():
        h = _norm_mod(x_ref[...], g_ref[...], mod_ref[1:2, :], mod_ref[0:1, :])
        h_ref[...] = h.astype(BF16)

    o_ref[...] = _dot(h_ref[...], w_ref[...])


def _norm_proj(geom, x, mod4, layer, norm_g, w_bf16, tn):
    t, d = x.shape
    n = w_bf16.shape[1]
    tm = ROW_TILE
    return pl.pallas_call(
        _proj_kernel,
        out_shape=jax.ShapeDtypeStruct((t, n), F32),
        grid=(t // tm, n // tn),
        in_specs=[
            pl.BlockSpec((tm, d), lambda i, j: (i, 0)),
            pl.BlockSpec((None, None, 6, d), lambda i, j: (layer, geom.seq_of_row(i * tm), 0, 0)),
            pl.BlockSpec((1, d), lambda i, j: (0, 0)),
            pl.BlockSpec((d, tn), lambda i, j: (0, j)),
        ],
        out_specs=pl.BlockSpec((tm, tn), lambda i, j: (i, j)),
        scratch_shapes=[pltpu.VMEM((tm, d), BF16)],
        compiler_params=pltpu.CompilerParams(
            dimension_semantics=("arbitrary", "arbitrary"), vmem_limit_bytes=VMEM_LIMIT),
        name="norm_proj",
    )(x, mod4, norm_g.reshape(1, d), w_bf16)


def _qkv_kernel(x_ref, mod_ref, g_ref, w_ref, o_ref, hs_ref, hv_ref, *, cols_per_branch):
    tm = x_ref.shape[0]
    j = pl.program_id(1)

    @pl.when(j == 0)
    def _():
        h = _norm_mod(x_ref[...], g_ref[...], mod_ref[1:2, :], mod_ref[0:1, :])
        n_lane_blocks = hs_ref.shape[0]
        lanes = hs_ref.shape[2]
        for c in range(n_lane_blocks):
            hs_ref[c] = h[:, c * lanes:(c + 1) * lanes]
        for g, (_, d) in enumerate(DIL_CONFIGS):
            if d == 1:
                hv_ref[g] = h.astype(BF16)
                continue
            per_class = tm // d
            for r in range(d):
                for c in range(n_lane_blocks):
                    hv_ref[g, pl.ds(r * per_class, per_class), c * lanes:(c + 1) * lanes] = (
                        hs_ref[c, pl.ds(r, per_class, stride=d), :].astype(BF16))

    tn = w_ref.shape[1]
    o_ref[...] = _dot(hv_ref[j // (cols_per_branch // tn)], w_ref[...]).astype(BF16)


def _qkv_proj(geom, x, mod4, layer, norm_g, w_bf16, tn):
    t, d = x.shape
    n = w_bf16.shape[1]
    tm = ATTN_SUPER
    n_br = len(DIL_CONFIGS)
    cols_per_branch = n // n_br
    assert cols_per_branch % tn == 0
    return pl.pallas_call(
        functools.partial(_qkv_kernel, cols_per_branch=cols_per_branch),
        out_shape=jax.ShapeDtypeStruct((t, n), BF16),
        grid=(t // tm, n // tn),
        in_specs=[
            pl.BlockSpec((tm, d), lambda i, j: (i, 0)),
            pl.BlockSpec((None, None, 6, d), lambda i, j: (layer, geom.seq_of_row(i * tm), 0, 0)),
            pl.BlockSpec((1, d), lambda i, j: (0, 0)),
            pl.BlockSpec((d, tn), lambda i, j: (0, j)),
        ],
        out_specs=pl.BlockSpec((tm, tn), lambda i, j: (i, j)),
        scratch_shapes=[pltpu.VMEM((d // LANES, tm, LANES), F32), pltpu.VMEM((n_br, tm, d), BF16)],
        compiler_params=pltpu.CompilerParams(
            dimension_semantics=("arbitrary", "arbitrary"), vmem_limit_bytes=VMEM_LIMIT),
        name="qkv_proj",
    )(x, mod4, norm_g.reshape(1, d), w_bf16)


def _t5_bucket(rel):
    half_b = NUM_BUCKETS // 2
    max_exact = half_b // 2
    n = jnp.abs(rel)
    large = max_exact + (jnp.log(jnp.maximum(n, 1).astype(F32) / max_exact)
                         / math.log(MAX_DISTANCE / max_exact) * (half_b - max_exact)).astype(I32)
    large = jnp.minimum(large, half_b - 1)
    return jnp.where(rel > 0, half_b, 0) + jnp.where(n < max_exact, n, large)


def _attn_bias_tables(rel_bias_table):
    half = ATTN_HALF
    qi = jnp.arange(half)[:, None]
    ki = jnp.arange(3 * half)[None, :]
    off = ki - half - qi
    in_band = jnp.abs(off) <= half
    per_branch = []
    for g, (window, dilation) in enumerate(DIL_CONFIGS):
        assert window // (2 * dilation) == half
        tab = rel_bias_table[:, g * HEADS_PER_GROUP:(g + 1) * HEADS_PER_GROUP]
        b = jnp.transpose(tab[_t5_bucket(off * dilation)], (2, 0, 1)).astype(F32)
        variants = []
        for var in range(4):
            ok = in_band
            if var & 1:
                ok = ok & (ki >= half)
            if var & 2:
                ok = ok & (ki < 2 * half)
            variants.append(jnp.where(ok[None], b, NEG))
        per_branch.append(jnp.stack(variants, axis=1))
    return jnp.stack(per_branch, axis=1)


def _attn_branch(g, d, q_ref, k_ref, v_ref, bias_ref, o_ref, mx_ref, wt_ref, os_ref, ls_ref,
                 sc_ref, pr_ref, dn_ref, lt_ref, seq_len):
    half = ATTN_HALF
    scale = HEAD_DIM ** -0.5
    units = ATTN_SUPER // half
    nbc = units // d
    n_sc = seq_len // ATTN_SUPER
    last = g == len(DIL_CONFIGS) - 1

    def rows(ref, start):
        return ref[pl.ds(pl.multiple_of(start, half), half), :]

    def super_chunk(sc, carry):
        base_sc = pl.multiple_of(sc * ATTN_SUPER, ATTN_SUPER)

        def neighbours(u):
            n = u % nbc
            at_start = n == 0
            at_end = n == nbc - 1
            prev = jnp.where(at_start, base_sc - ATTN_SUPER + (u + nbc - 1) * half, base_sc + (u - 1) * half)
            nxt = jnp.where(at_end, base_sc + ATTN_SUPER + (u - nbc + 1) * half, base_sc + (u + 1) * half)
            no_prev = jnp.logical_and(at_start, sc == 0)
            no_next = jnp.logical_and(at_end, sc == n_sc - 1)
            prev = jnp.where(no_prev, base_sc + u * half, prev)
            nxt = jnp.where(no_next, base_sc + u * half, nxt)
            return prev, nxt, no_prev.astype(I32) + 2 * no_next.astype(I32)

        def window(ref, u):
            prev, nxt, var = neighbours(u)
            return jnp.concatenate([rows(ref, prev), rows(ref, base_sc + u * half), rows(ref, nxt)], axis=0), var

        def score_unit(u, c2):
            kk, var = window(k_ref, u)
            q = rows(q_ref, base_sc + u * half)
            s = lax.dot_general(q, kk, NT_DIMS, preferred_element_type=F32) * scale + bias_ref[g, var]
            sc_ref[pl.ds(pl.multiple_of(u * half, half), half), :] = s
            return c2

        lax.fori_loop(0, units, score_unit, 0, unroll=ATTN_UNROLL)

        s = sc_ref[...]
        m = jnp.max(s, axis=-1, keepdims=True)
        p = jnp.exp(s - m)
        den = jnp.sum(p, axis=-1, keepdims=True)
        pr_ref[...] = p.astype(BF16)
        dn_ref[...] = jnp.broadcast_to(den, dn_ref.shape)
        lt_ref[...] = jnp.broadcast_to(m + jnp.log(den), lt_ref.shape)

        def value_unit(u, c2):
            vv, _ = window(v_ref, u)
            blk = pl.ds(pl.multiple_of(u * half, half), half)
            o = _dot(pr_ref[blk, :], vv) / dn_ref[blk, :]
            nat = pl.ds((u % nbc) * (half * d) + u // nbc, half, stride=d)
            os_ref[nat, :] = o
            ls_ref[nat, :] = lt_ref[blk, :]
            return c2

        lax.fori_loop(0, units, value_unit, 0, unroll=ATTN_UNROLL)

        span = pl.ds(base_sc, ATTN_SUPER)
        og, lg = os_ref[...], ls_ref[...]
        if g == 0:
            o_ref[span, :] = og
            mx_ref[span, :] = lg
            wt_ref[span, :] = jnp.ones_like(lg)
        else:
            m_old = mx_ref[span, :]
            m_new = jnp.maximum(m_old, lg)
            a = jnp.exp(m_old - m_new)
            b = jnp.exp(lg - m_new)
            acc = o_ref[span, :] * a + og * b
            wt = wt_ref[span, :] * a + b
            if last:
                o_ref[span, :] = acc / wt
            else:
                o_ref[span, :] = acc
                mx_ref[span, :] = m_new
                wt_ref[span, :] = wt
        return carry

    lax.fori_loop(0, seq_len // ATTN_SUPER, super_chunk, 0)


def _attn_kernel(q_ref, k_ref, v_ref, bias_ref, *rest, seq_len):
    n_scratch = 8
    o_ref = rest[-n_scratch - 1]
    for g, (_, d) in enumerate(DIL_CONFIGS):
        @pl.when(pl.program_id(2) == g)
        def _(g=g, d=d):
            _attn_branch(g, d, q_ref, k_ref, v_ref, bias_ref, o_ref, *rest[-n_scratch:], seq_len)


def _attention_group(qkv, bias, prev_out, *, n_seq, seq_len, row_block0, t_total):
    assert seq_len % ATTN_SUPER == 0
    for _, d in DIL_CONFIGS:
        assert ATTN_SUPER % (ATTN_HALF * d) == 0
    h = HEADS_PER_GROUP
    width = h * HEAD_DIM
    n_br = len(DIL_CONFIGS)

    def col_spec(c):
        return pl.BlockSpec((seq_len, HEAD_DIM), lambda b, hh, g: (row_block0 + b, (g * 3 + c) * h + hh))

    in_specs = [col_spec(c) for c in range(3)]
    in_specs.append(pl.BlockSpec((None, n_br, 4, ATTN_HALF, 3 * ATTN_HALF),
                                 lambda b, hh, g: (hh, 0, 0, 0, 0)))
    args = [qkv] * 3 + [bias]
    aliases = {}
    if prev_out is not None:
        in_specs.append(pl.BlockSpec(memory_space=pl.ANY))
        args.append(prev_out)
        aliases = {len(args) - 1: 0}
    return pl.pallas_call(
        functools.partial(_attn_kernel, seq_len=seq_len),
        out_shape=jax.ShapeDtypeStruct((t_total, width), F32),
        grid=(n_seq, h, n_br),
        in_specs=in_specs,
        out_specs=pl.BlockSpec((seq_len, HEAD_DIM), lambda b, hh, g: (row_block0 + b, hh)),
        scratch_shapes=[pltpu.VMEM((seq_len, HEAD_DIM), F32),
                        pltpu.VMEM((seq_len, HEAD_DIM), F32),
                        pltpu.VMEM((ATTN_SUPER, HEAD_DIM), F32),
                        pltpu.VMEM((ATTN_SUPER, HEAD_DIM), F32),
                        pltpu.VMEM((ATTN_SUPER, 3 * ATTN_HALF), F32),
                        pltpu.VMEM((ATTN_SUPER, 3 * ATTN_HALF), BF16),
                        pltpu.VMEM((ATTN_SUPER, HEAD_DIM), F32),
                        pltpu.VMEM((ATTN_SUPER, HEAD_DIM), F32)],
        input_output_aliases=aliases,
        compiler_params=pltpu.CompilerParams(
            dimension_semantics=("arbitrary", "arbitrary", "arbitrary"), vmem_limit_bytes=VMEM_LIMIT),
        name="dilated_attention",
    )(*args)


def _route(h2, wrh_ref, wrl_ref, rb_ref, utri_ref, run_ref, te_ref, gt_ref, rk_ref, cnt_ref):
    n_exp = wrh_ref.shape[0]
    tm = h2.shape[0]
    per_group = n_exp // N_EXPERT_GROUPS
    h_hi, h_lo = _split_bf16(h2)
    wrh = wrh_ref[...]
    logits = (lax.dot_general(wrh, h_hi, NT_DIMS, preferred_element_type=F32)
              + lax.dot_general(wrh, h_lo, NT_DIMS, preferred_element_type=F32)
              + lax.dot_general(wrl_ref[...], h_hi, NT_DIMS, preferred_element_type=F32))
    scores = jax.nn.sigmoid(logits)
    biased = scores + rb_ref[...]

    gi = lax.broadcasted_iota(I32, (per_group, tm), 0).astype(F32)
    group_vals, group_scores = [], []
    for g in range(N_EXPERT_GROUPS):
        v = biased[g * per_group:(g + 1) * per_group, :]
        m1 = jnp.max(v, axis=0, keepdims=True)
        i1 = jnp.min(jnp.where(v == m1, gi, float(per_group)), axis=0, keepdims=True)
        m2 = jnp.max(jnp.where(gi == i1, -jnp.inf, v), axis=0, keepdims=True)
        group_vals.append(v)
        group_scores.append(m1 + m2)
    gs = jnp.concatenate(group_scores, axis=0)
    g_iota = lax.broadcasted_iota(I32, gs.shape, 0)
    rank = jnp.zeros(gs.shape, I32)
    for gp in range(N_EXPERT_GROUPS):
        row = gs[gp:gp + 1, :]
        beats = jnp.where(row > gs, 1, jnp.where(row == gs, jnp.where(g_iota > gp, 1, 0), 0))
        rank = rank + beats
    keep = rank < TOPK_GROUPS
    cur = jnp.concatenate(
        [jnp.where(keep[g:g + 1, :], group_vals[g], NEG) for g in range(N_EXPERT_GROUPS)], axis=0)

    e_iota = lax.broadcasted_iota(I32, (n_exp, tm), 0).astype(F32)
    picks, gates = [], []
    for _ in range(TOP_K):
        mk = jnp.max(cur, axis=0, keepdims=True)
        ik = jnp.min(jnp.where(cur == mk, e_iota, float(n_exp)), axis=0, keepdims=True)
        sel = e_iota == ik
        gates.append(jnp.sum(jnp.where(sel, scores, 0.0), axis=0, keepdims=True))
        cur = jnp.where(sel, -jnp.inf, cur)
        picks.append(ik)
    gt = jnp.concatenate(gates, axis=0)
    gt = gt / jnp.sum(gt, axis=0, keepdims=True) * ROUTED_SCALE
    te_ref[...] = jnp.concatenate(picks, axis=0).astype(I32)
    gt_ref[...] = gt

    hot = jnp.where(cur == -jnp.inf, 1.0, 0.0)
    before = _dot(hot.astype(BF16), utri_ref[...]) + run_ref[...]
    ranks = [jnp.sum(jnp.where(e_iota == picks[k], before, 0.0), axis=0, keepdims=True)
             for k in range(TOP_K)]
    rk_ref[...] = jnp.concatenate(ranks, axis=0).astype(I32)
    run_ref[...] = run_ref[...] + jnp.sum(hot, axis=1, keepdims=True)
    cnt_ref[...] = run_ref[...]


def _post_common(mix, x_ref, mod_ref, n2_ref, wrh_ref, wrl_ref, rb_ref, utri_ref,
                 x1_ref, h2_ref, te_ref, gt_ref, rk_ref, cnt_ref, run_ref):
    @pl.when(pl.program_id(0) == 0)
    def _():
        run_ref[...] = jnp.zeros(run_ref.shape, F32)

    x1 = x_ref[...] + mod_ref[2:3, :] * mix
    x1_ref[...] = x1
    h2 = _norm_mod(x1, n2_ref[...], mod_ref[4:5, :], mod_ref[3:4, :])
    _store_row_major(h2_ref, 0, _pack_bf16_pair(h2))
    _route(h2, wrh_ref, wrl_ref, rb_ref, utri_ref, run_ref, te_ref, gt_ref, rk_ref, cnt_ref)


def _post_attn_kernel(a_ref, wo_ref, *rest):
    mix = _dot(a_ref[...].astype(BF16), wo_ref[...])
    _post_common(mix, *rest)


def _post_pool_kernel(u_ref, up_ref, un_ref, wg_ref, cs_ref, wo_ref, *rest, geom):
    ext_ref = rest[-1]
    rest = rest[:-1]
    tm, d = u_ref.shape
    halo = POOL_HALO
    row0 = pl.program_id(0) * tm
    pos0, slen = geom.pos_and_len(row0)
    u = u_ref[...]
    ext_ref[pl.ds(halo, tm), :] = u
    ext_ref[pl.ds(0, halo), :] = jnp.where(pos0 > 0, up_ref[...], 0.0)
    ext_ref[pl.ds(halo + tm, halo), :] = jnp.where(pos0 + tm < slen, un_ref[...], 0.0)
    pos = pos0 + lax.broadcasted_iota(I32, (tm, 1), 0)
    gc = d // len(POOL_WINDOWS)
    zs = []
    for g, w in enumerate(POOL_WINDOWS):
        hw = w // 2
        assert hw <= halo
        cols = slice(g * gc, (g + 1) * gc)
        acc = ext_ref[pl.ds(halo - hw, tm), cols]
        for o in range(-hw + 1, hw):
            acc = acc + ext_ref[pl.ds(halo + o, tm), cols]
        cnt = (jnp.minimum(pos + hw, slen) - jnp.maximum(pos - hw, 0)).astype(F32)
        p = acc / cnt - u[:, cols]
        zs.append(_dot(p.astype(BF16), wg_ref[g]) * cs_ref[:, cols])
    z = jnp.concatenate(zs, axis=1)
    mix = _dot(z.astype(BF16), wo_ref[...])
    _post_common(mix, *rest)


def _post_mixer(geom, layer, mixer_args, mixer_specs, body, x, mod4, norm2_g, w_router, router_bias,
                extra_scratch=()):
    t, d = x.shape
    n_exp = w_router.shape[1]
    tm = ROW_TILE
    wr_t = w_router.T
    wr_hi = wr_t.astype(BF16)
    wr_lo = (wr_t - wr_hi.astype(F32)).astype(BF16)
    utri = (jnp.arange(tm)[:, None] < jnp.arange(tm)[None, :]).astype(BF16)
    row = lambda i: (i, 0)
    col = lambda i: (0, i)
    const = lambda i: (0, 0)
    in_specs = list(mixer_specs) + [
        pl.BlockSpec((tm, d), row),
        pl.BlockSpec((None, None, 6, d), lambda i: (layer, geom.seq_of_row(i * tm), 0, 0)),
        pl.BlockSpec((1, d), const),
        pl.BlockSpec((n_exp, d), const),
        pl.BlockSpec((n_exp, d), const),
        pl.BlockSpec((n_exp, 1), const),
        pl.BlockSpec((tm, tm), const),
    ]
    out_shape = [
        jax.ShapeDtypeStruct((t, d), F32),
        jax.ShapeDtypeStruct((t * (d // 2 // LANES), LANES), U32),
        jax.ShapeDtypeStruct((TOP_K, t), I32),
        jax.ShapeDtypeStruct((TOP_K, t), F32),
        jax.ShapeDtypeStruct((TOP_K, t), I32),
        jax.ShapeDtypeStruct((n_exp, 1), F32),
    ]
    out_specs = [
        pl.BlockSpec((tm, d), row),
        pl.BlockSpec((tm * (d // 2 // LANES), LANES), row),
        pl.BlockSpec((TOP_K, tm), col),
        pl.BlockSpec((TOP_K, tm), col),
        pl.BlockSpec((TOP_K, tm), col),
        pl.BlockSpec((n_exp, 1), const),
    ]
    return pl.pallas_call(
        body,
        out_shape=out_shape,
        grid=(t // tm,),
        in_specs=in_specs,
        out_specs=out_specs,
        scratch_shapes=[pltpu.VMEM((n_exp, 1), F32)] + list(extra_scratch),
        compiler_params=pltpu.CompilerParams(
            dimension_semantics=("arbitrary",), vmem_limit_bytes=VMEM_LIMIT),
        name="post_mixer",
    )(*mixer_args, x, mod4, norm2_g.reshape(1, d), wr_hi, wr_lo, router_bias.reshape(n_exp, 1), utri)


def _dest_kernel(te_ref, rk_ref, ps_ref, d_ref):
    n_exp = ps_ref.shape[0]
    te = te_ref[...]
    e_iota = lax.broadcasted_iota(I32, (n_exp, te.shape[1]), 0)
    ps = ps_ref[...]
    rows = [jnp.sum(jnp.where(e_iota == te[k:k + 1, :], ps, 0.0), axis=0, keepdims=True)
            for k in range(TOP_K)]
    d_ref[...] = jnp.concatenate(rows, axis=0).astype(I32) + rk_ref[...]


def _dest_slots(te, rk, pstart):
    k, t = te.shape
    n_exp = pstart.shape[0]
    tm = ROW_TILE
    col = lambda i: (0, i)
    return pl.pallas_call(
        _dest_kernel,
        out_shape=jax.ShapeDtypeStruct((k, t), I32),
        grid=(t // tm,),
        in_specs=[pl.BlockSpec((k, tm), col), pl.BlockSpec((k, tm), col),
                  pl.BlockSpec((n_exp, 1), lambda i: (0, 0))],
        out_specs=pl.BlockSpec((k, tm), col),
        name="moe_dest",
    )(te, rk, pstart.astype(F32).reshape(n_exp, 1))


def _zero_kernel(lb_ref, o_ref):
    o_ref[...] = jnp.zeros(o_ref.shape, o_ref.dtype)


def _zero_tail_blocks(last_blk, p_rows, pieces):
    return pl.pallas_call(
        _zero_kernel,
        out_shape=jax.ShapeDtypeStruct((p_rows, pieces, LANES), U32),
        grid_spec=pltpu.PrefetchScalarGridSpec(
            num_scalar_prefetch=1,
            grid=(last_blk.shape[0],),
            in_specs=[],
            out_specs=pl.BlockSpec((EXPERT_BLOCK, pieces, LANES), lambda e, lb: (lb[e], 0, 0)),
        ),
        name="moe_zero_tails",
    )(last_blk)


def _dispatch_kernel(dest_ref, h_ref, xs_in_ref, xs_ref, sem):
    del xs_in_ref
    pieces = xs_ref.shape[1]
    tt = h_ref.shape[0] // pieces

    def issue(tb, carry):
        t0 = pl.multiple_of(tb * SUBLANES, SUBLANES)
        for j in range(SUBLANES):
            for k in range(TOP_K):
                pltpu.async_copy(h_ref.at[pl.ds((t0 + j) * pieces, pieces), :],
                                 xs_ref.at[dest_ref[(t0 + j) * TOP_K + k]], sem,
                                 priority=k % DMA_QUEUES)
        return carry

    lax.fori_loop(0, tt // SUBLANES, issue, 0)
    for k in range(TOP_K):
        pltpu.make_async_copy(h_ref, h_ref, sem).wait()


def _dispatch(dest, h2, xs_zeroed):
    pieces = xs_zeroed.shape[1]
    t = h2.shape[0] // pieces
    tt = MOE_TOKENS
    return pl.pallas_call(
        _dispatch_kernel,
        out_shape=jax.ShapeDtypeStruct(xs_zeroed.shape, xs_zeroed.dtype),
        grid=(t // tt,),
        in_specs=[
            pl.BlockSpec((TOP_K * tt,), lambda i: (i,), memory_space=pltpu.SMEM),
            pl.BlockSpec((tt * pieces, LANES), lambda i: (i, 0)),
            pl.BlockSpec(memory_space=pl.ANY),
        ],
        out_specs=pl.BlockSpec(memory_space=pl.ANY),
        scratch_shapes=[pltpu.SemaphoreType.DMA(())],
        input_output_aliases={2: 0},
        compiler_params=pltpu.CompilerParams(
            dimension_semantics=("arbitrary",), has_side_effects=True),
        name="moe_dispatch",
    )(dest, h2, xs_zeroed)


def _expert_kernel(be_ref, nu_ref, x_ref, wg_ref, wu_ref, wd_ref, o_ref, wgb_ref, wub_ref, wdb_ref):
    i = pl.program_id(0)

    @pl.when(i < nu_ref[0])
    def _():
        @pl.when(jnp.logical_or(i == 0, be_ref[i] != be_ref[jnp.maximum(i - 1, 0)]))
        def _():
            wgb_ref[...] = wg_ref[...].astype(BF16)
            wub_ref[...] = wu_ref[...].astype(BF16)
            wdb_ref[...] = wd_ref[...].astype(BF16)

        half = wgb_ref.shape[0] // 2
        pieces = half // LANES
        for c in range(EXPERT_BLOCK // EXPERT_CHUNK):
            lo, hi = _unpack_bf16_pair(_load_row_major(x_ref, c * EXPERT_CHUNK, EXPERT_CHUNK, pieces))
            lo, hi = lo.astype(BF16), hi.astype(BF16)

            def proj(w_ref, lo=lo, hi=hi):
                return _dot(lo, w_ref[pl.ds(0, half), :]) + _dot(hi, w_ref[pl.ds(half, half), :])

            hid = _silu(proj(wgb_ref)) * proj(wub_ref)
            _store_row_major(o_ref, c * EXPERT_CHUNK, _pack_bf16_pair(_dot(hid.astype(BF16), wdb_ref[...])))


def _experts(layer, blk_expert, n_used, xs, wg, wu, wd):
    d, de = wg.shape[2], wg.shape[3]
    pieces = d // 2 // LANES
    p_rows = xs.shape[0] // pieces
    blk = lambda i, be, nu: (jnp.minimum(i, nu[0] - 1), 0)
    wsel = lambda i, be, nu: (layer, be[i], 0, 0)
    return pl.pallas_call(
        _expert_kernel,
        out_shape=jax.ShapeDtypeStruct(xs.shape, U32),
        grid_spec=pltpu.PrefetchScalarGridSpec(
            num_scalar_prefetch=2,
            grid=(p_rows // EXPERT_BLOCK,),
            in_specs=[
                pl.BlockSpec((EXPERT_BLOCK * pieces, LANES), blk),
                pl.BlockSpec((None, None, d, de), wsel),
                pl.BlockSpec((None, None, d, de), wsel),
                pl.BlockSpec((None, None, de, d), wsel),
            ],
            out_specs=pl.BlockSpec((EXPERT_BLOCK * pieces, LANES), blk),
            scratch_shapes=[pltpu.VMEM((d, de), BF16), pltpu.VMEM((d, de), BF16), pltpu.VMEM((de, d), BF16)],
        ),
        compiler_params=pltpu.CompilerParams(
            dimension_semantics=("arbitrary",), vmem_limit_bytes=VMEM_LIMIT),
        name="moe_experts",
    )(blk_expert, n_used, xs, wg, wu, wd)


def _combine_kernel(dest_ref, dnext_ref, x1_ref, h_ref, gt_ref, mod_ref, wsg_ref, wsu_ref, wsd_ref, fg_ref,
                    ys_ref, *rest, final_norm, first_steps):
    buf_ref, sems = rest[-2:]
    out_refs = rest[:-2]
    tt = x1_ref.shape[0]
    half = x1_ref.shape[1] // 2
    pieces = half // LANES
    i = pl.program_id(0)
    slot = i % 2

    def issue(d_ref, s):
        def body(tb, carry):
            t0 = pl.multiple_of(tb * SUBLANES, SUBLANES)
            for j in range(SUBLANES):
                for k in range(TOP_K):
                    pltpu.async_copy(ys_ref.at[d_ref[(t0 + j) * TOP_K + k]],
                                     buf_ref.at[s, k, pl.ds((t0 + j) * pieces, pieces), :], sems.at[s],
                                     priority=k % DMA_QUEUES)
            return carry

        lax.fori_loop(0, tt // SUBLANES, body, 0)

    @pl.when(i == 0)
    def _():
        issue(dest_ref, 0)

    for s in range(2):
        @pl.when(jnp.logical_and(i + 1 < pl.num_programs(0), slot == 1 - s))
        def _(s=s):
            issue(dnext_ref, s)

    lo, hi = _unpack_bf16_pair(_load_row_major(h_ref, 0, tt, pieces))
    lo, hi = lo.astype(BF16), hi.astype(BF16)

    def proj(w_ref):
        return _dot(lo, w_ref[pl.ds(0, half), :]) + _dot(hi, w_ref[pl.ds(half, half), :])

    hid = _silu(proj(wsg_ref)) * proj(wsu_ref)
    y = _dot(hid.astype(BF16), wsd_ref[...])
    for k in range(TOP_K):
        pltpu.make_async_copy(buf_ref.at[slot, k], buf_ref.at[slot, k], sems.at[slot]).wait()
    gt = gt_ref[...]
    acc_lo = jnp.zeros((tt, half), F32)
    acc_hi = jnp.zeros((tt, half), F32)
    for k in range(TOP_K):
        e_lo, e_hi = _unpack_bf16_pair(_load_row_major(buf_ref.at[slot, k], 0, tt, pieces))
        acc_lo = acc_lo + e_lo * gt[:, k:k + 1]
        acc_hi = acc_hi + e_hi * gt[:, k:k + 1]
    y = y + jnp.concatenate([acc_lo, acc_hi], axis=1)
    x2 = x1_ref[...] + mod_ref[5:6, :] * y
    if not final_norm:
        out_refs[0][...] = x2
        return
    ms = jnp.mean(x2 * x2, axis=-1, keepdims=True)
    x2 = x2 * lax.rsqrt(ms + EPS) * fg_ref[...]

    @pl.when(i < first_steps)
    def _():
        out_refs[0][...] = x2

    @pl.when(i >= first_steps)
    def _():
        out_refs[1][...] = x2


def _combine(geom, layer, dest, x1, h2, gate_t, mod4, wsg, wsu, wsd, final_g, ys, final_norm):
    t, d = x1.shape
    de = wsg.shape[1]
    tt = MOE_TOKENS
    n_steps = t // tt
    first_steps = geom.tp // tt
    row = lambda i: (i, 0)
    const = lambda i: (0, 0)
    if final_norm:
        out_shape = [jax.ShapeDtypeStruct((geom.tp, d), F32), jax.ShapeDtypeStruct((geom.ts, d), F32)]
        out_specs = [pl.BlockSpec((tt, d), lambda i: (jnp.minimum(i, first_steps - 1), 0)),
                     pl.BlockSpec((tt, d), lambda i: (jnp.maximum(i - first_steps, 0), 0))]
    else:
        out_shape = jax.ShapeDtypeStruct((t, d), F32)
        out_specs = pl.BlockSpec((tt, d), row)
    return pl.pallas_call(
        functools.partial(_combine_kernel, final_norm=final_norm, first_steps=first_steps),
        out_shape=out_shape,
        grid=(n_steps,),
        in_specs=[
            pl.BlockSpec((TOP_K * tt,), lambda i: (i,), memory_space=pltpu.SMEM),
            pl.BlockSpec((TOP_K * tt,), lambda i: (jnp.minimum(i + 1, n_steps - 1),), memory_space=pltpu.SMEM),
            pl.BlockSpec((tt, d), row),
            pl.BlockSpec((tt * (d // 2 // LANES), LANES), row),
            pl.BlockSpec((tt, TOP_K), row),
            pl.BlockSpec((None, None, 6, d), lambda i: (layer, geom.seq_of_row(i * tt), 0, 0)),
            pl.BlockSpec((d, de), const),
            pl.BlockSpec((d, de), const),
            pl.BlockSpec((de, d), const),
            pl.BlockSpec((1, d), const),
            pl.BlockSpec(memory_space=pl.ANY),
        ],
        out_specs=out_specs,
        scratch_shapes=[pltpu.VMEM((2, TOP_K, tt * (d // 2 // LANES), LANES), U32), pltpu.SemaphoreType.DMA((2,))],
        compiler_params=pltpu.CompilerParams(
            dimension_semantics=("arbitrary",), vmem_limit_bytes=VMEM_LIMIT),
        name="moe_combine",
    )(dest, dest, x1, h2, gate_t, mod4, wsg, wsu, wsd, final_g.reshape(1, d), ys)


def _moe(geom, layer, x1, h2, te, gt, rk, cnt, mod4, wg, wu, wd, wsg, wsu, wsd, final_g, final_norm):
    t, d = x1.shape
    n_exp = wg.shape[1]
    p_rows = t * TOP_K + n_exp * EXPERT_BLOCK
    n_blk = p_rows // EXPERT_BLOCK
    counts = cnt[:, 0].astype(I32)
    pcounts = (counts + EXPERT_BLOCK - 1) // EXPERT_BLOCK * EXPERT_BLOCK
    pend = jnp.cumsum(pcounts)
    pstart = pend - pcounts
    n_used = pend[-1] // EXPERT_BLOCK
    blk_ids = jnp.arange(n_blk, dtype=I32)
    be = jnp.sum((pend[None, :] <= (blk_ids * EXPERT_BLOCK)[:, None]).astype(I32), axis=1)
    be = jnp.minimum(be, jnp.minimum(be[n_used - 1], n_exp - 1))
    last_blk = jnp.maximum(pend // EXPERT_BLOCK - 1, 0).astype(I32)

    dest = _dest_slots(te, rk, pstart).T.reshape(-1)
    pieces = d // 2 // LANES
    xs = _dispatch(dest, h2, _zero_tail_blocks(last_blk, p_rows, pieces))
    ys = _experts(layer, be, n_used.reshape(1).astype(I32), xs.reshape(p_rows * pieces, LANES), wg, wu, wd)
    ys = ys.reshape(p_rows, pieces, LANES)
    return _combine(geom, layer, dest, x1, h2, gt.T, mod4, wsg, wsu, wsd, final_g, ys, final_norm)


def kernel(x_prompt, x_sample, c_prompt, c_sample, w_mod, b_mod, norm1_g, norm2_g, rel_bias_table, w_in_a, w_out_a, w_in_b, w_grp_b, pool_scale_b, w_out_b, w_router, router_bias, w_gate_e, w_up_e, w_down_e, w_gate_s, w_up_s, w_down_s, final_norm_g):
    bp, sp, d = x_prompt.shape
    bs, ss, _ = x_sample.shape
    geom = _Geom(bp, sp, bs, ss)
    depth = w_mod.shape[0]
    tm = ROW_TILE
    assert sp % ATTN_SUPER == 0 and ss % ATTN_SUPER == 0 and geom.tp % ss == 0
    assert sp % tm == 0 and ss % tm == 0 and tm % MOE_TOKENS == 0

    x = jnp.concatenate([x_prompt.reshape(geom.tp, d), x_sample.reshape(geom.ts, d)], axis=0)
    c = jnp.concatenate([c_prompt, c_sample], axis=0)
    mod4 = _modulation(c, w_mod, b_mod).reshape(depth, geom.nb, 6, d)
    bias = _attn_bias_tables(rel_bias_table)

    for i in range(depth):
        j = i // 2
        if i % 2 == 0:
            qkv = _qkv_proj(geom, x, mod4, i, norm1_g[i], w_in_a[j].astype(BF16), 1536)
            attn = _attention_group(qkv, bias, None, n_seq=bp, seq_len=sp, row_block0=0, t_total=geom.t)
            attn = _attention_group(qkv, bias, attn, n_seq=bs, seq_len=ss, row_block0=geom.tp // ss,
                                    t_total=geom.t)
            mixer_args = (attn, w_out_a[j].astype(BF16))
            mixer_specs = (pl.BlockSpec((tm, d), lambda r: (r, 0)),
                           pl.BlockSpec((d, d), lambda r: (0, 0)))
            body, extra = _post_attn_kernel, ()
        else:
            u = _norm_proj(geom, x, mod4, i, norm1_g[i], w_in_b[j].astype(BF16), 1024)
            hb = tm // POOL_HALO
            n_hb = geom.t // POOL_HALO
            mixer_args = (u, u, u, w_grp_b[j].astype(BF16), pool_scale_b[j].reshape(1, d),
                          w_out_b[j].astype(BF16))
            gc = d // len(POOL_WINDOWS)
            mixer_specs = (pl.BlockSpec((tm, d), lambda r: (r, 0)),
                           pl.BlockSpec((POOL_HALO, d), lambda r: (jnp.maximum(r * hb - 1, 0), 0)),
                           pl.BlockSpec((POOL_HALO, d), lambda r: (jnp.minimum((r + 1) * hb, n_hb - 1), 0)),
                           pl.BlockSpec((len(POOL_WINDOWS), gc, gc), lambda r: (0, 0, 0)),
                           pl.BlockSpec((1, d), lambda r: (0, 0)),
                           pl.BlockSpec((d, d), lambda r: (0, 0)))
            body = functools.partial(_post_pool_kernel, geom=geom)
            extra = (pltpu.VMEM((tm + 2 * POOL_HALO, d), F32),)
        x1, h2, te, gt, rk, cnt = _post_mixer(geom, i, mixer_args, mixer_specs, body, x, mod4, norm2_g[i],
                                              w_router[i], router_bias[i], extra)
        x = _moe(geom, i, x1, h2, te, gt, rk, cnt, mod4,
                 w_gate_e, w_up_e, w_down_e,
                 w_gate_s[i].astype(BF16), w_up_s[i].astype(BF16), w_down_s[i].astype(BF16),
                 final_norm_g, final_norm=(i == depth - 1))

    y_prompt, y_sample = x
    return (y_prompt.reshape(bp, sp, d), y_sample.reshape(bs, ss, d))
```

```python
import functools
import math

import jax
import jax.numpy as jnp
from jax import lax
from jax.experimental import pallas as pl
from jax.experimental.pallas import tpu as pltpu

F32 = jnp.float32
BF16 = jnp.bfloat16
I32 = jnp.int32

EPS = 1e-6
NEG = -1e30

DIL_CONFIGS = ((128, 1), (512, 4), (2048, 16))
HEADS_PER_GROUP = 8
HEAD_DIM = 128
NUM_BUCKETS = 32
MAX_DISTANCE = 1024
POOL_WINDOWS = (2, 4, 8, 16)
TOP_K = 8
N_EXPERT_GROUPS = 8
TOPK_GROUPS = 4
ROUTED_SCALE = 2.5

LANES = 128
SUBLANES = 8
DMA_QUEUES = 2
ROW_TILE = 512
EXPERT_BLOCK = 512
EXPERT_CHUNK = 512
MOE_TOKENS = 512
ATTN_SUPER = 1024
ATTN_HALF = 64
ATTN_UNROLL = 16
POOL_HALO = 8
VMEM_LIMIT = 56 * 1024 * 1024

NT_DIMS = (((1,), (1,)), ((), ()))


def _dot(a, b):
    return jnp.dot(a, b, preferred_element_type=F32)


def _split_bf16(a):
    hi = a.astype(BF16)
    lo = (a - hi.astype(F32)).astype(BF16)
    return hi, lo


def _norm_mod(x, g, sc, sh):
    ms = jnp.mean(x * x, axis=-1, keepdims=True)
    y = x * lax.rsqrt(ms + EPS) * g
    return y * (1.0 + sc) + sh


def _silu(x):
    return x * jax.nn.sigmoid(x)


U32 = jnp.uint32
_HI16 = 0xFFFF0000


def _pack_bf16_pair(x):
    n = x.shape[1] // 2
    r = lax.bitcast_convert_type(x.astype(BF16).astype(F32), U32)
    return lax.shift_right_logical(r[:, :n], jnp.uint32(16)) | (r[:, n:] & jnp.uint32(_HI16))


def _store_row_major(ref, row0, packed):
    n, w = packed.shape
    pieces = w // LANES
    for p in range(pieces):
        ref[pl.ds(row0 * pieces + p, n, stride=pieces), :] = packed[:, p * LANES:(p + 1) * LANES]


def _load_row_major(ref, row0, n, pieces):
    return jnp.concatenate([ref[pl.ds(row0 * pieces + p, n, stride=pieces), :] for p in range(pieces)], axis=1)


def _unpack_bf16_pair(p):
    lo = lax.bitcast_convert_type(lax.shift_left(p, jnp.uint32(16)), F32)
    hi = lax.bitcast_convert_type(p & jnp.uint32(_HI16), F32)
    return lo, hi


class _Geom:
    def __init__(self, bp, sp, bs, ss):
        self.bp, self.sp, self.bs, self.ss = bp, sp, bs, ss
        self.tp, self.ts = bp * sp, bs * ss
        self.t = self.tp + self.ts
        self.nb = bp + bs

    def seq_of_row(self, row):
        return jnp.where(row < self.tp, row // self.sp, self.bp + (row - self.tp) // self.ss)

    def pos_and_len(self, row):
        in_p = row < self.tp
        pos = jnp.where(in_p, row % self.sp, (row - self.tp) % self.ss)
        return pos, jnp.where(in_p, self.sp, self.ss)


def _mod_kernel(c_ref, w_ref, b_ref, o_ref):
    a_hi, a_lo = _split_bf16(_silu(c_ref[...]))
    w_hi, w_lo = _split_bf16(w_ref[...])
    o_ref[...] = _dot(a_hi, w_hi) + _dot(a_hi, w_lo) + _dot(a_lo, w_hi) + b_ref[...]


def _modulation(c, w_mod, b_mod):
    depth, d, n = w_mod.shape
    nb = c.shape[0]
    tn = 1024
    return pl.pallas_call(
        _mod_kernel,
        out_shape=jax.ShapeDtypeStruct((depth, nb, n), F32),
        grid=(depth, n // tn),
        in_specs=[
            pl.BlockSpec((nb, d), lambda l, j: (0, 0)),
            pl.BlockSpec((None, d, tn), lambda l, j: (l, 0, j)),
            pl.BlockSpec((None, 1, tn), lambda l, j: (l, 0, j)),
        ],
        out_specs=pl.BlockSpec((None, nb, tn), lambda l, j: (l, 0, j)),
        compiler_params=pltpu.CompilerParams(vmem_limit_bytes=VMEM_LIMIT),
        name="modulation",
    )(c, w_mod, b_mod.reshape(depth, 1, n))


def _proj_kernel(x_ref, mod_ref, g_ref, w_ref, o_ref, h_ref):
    @pl.when(pl.program_id(1) == 0)
    def _():
        h = _norm_mod(x_ref[...], g_ref[...], mod_ref[1:2, :], mod_ref[0:1, :])
        h_ref[...] = h.astype(BF16)

    o_ref[...] = _dot(h_ref[...], w_ref[...])


def _norm_proj(geom, x, mod4, layer, norm_g, w_bf16, tn):
    t, d = x.shape
    n = w_bf16.shape[1]
    tm = ROW_TILE
    return pl.pallas_call(
        _proj_kernel,
        out_shape=jax.ShapeDtypeStruct((t, n), F32),
        grid=(t // tm, n // tn),
        in_specs=[
            pl.BlockSpec((tm, d), lambda i, j: (i, 0)),
            pl.BlockSpec((None, None, 6, d), lambda i, j: (layer, geom.seq_of_row(i * tm), 0, 0)),
            pl.BlockSpec((1, d), lambda i, j: (0, 0)),
            pl.BlockSpec((d, tn), lambda i, j: (0, j)),
        ],
        out_specs=pl.BlockSpec((tm, tn), lambda i, j: (i, j)),
        scratch_shapes=[pltpu.VMEM((tm, d), BF16)],
        compiler_params=pltpu.CompilerParams(
            dimension_semantics=("arbitrary", "arbitrary"), vmem_limit_bytes=VMEM_LIMIT),
        name="norm_proj",
    )(x, mod4, norm_g.reshape(1, d), w_bf16)


def _qkv_kernel(x_ref, mod_ref, g_ref, w_ref, o_ref, hs_ref, hv_ref, *, cols_per_branch):
    tm = x_ref.shape[0]
    j = pl.program_id(1)

    @pl.when(j == 0)
    def _():
        h = _norm_mod(x_ref[...], g_ref[...], mod_ref[1:2, :], mod_ref[0:1, :])
        n_lane_blocks = hs_ref.shape[0]
        lanes = hs_ref.shape[2]
        for c in range(n_lane_blocks):
            hs_ref[c] = h[:, c * lanes:(c + 1) * lanes]
        for g, (_, d) in enumerate(DIL_CONFIGS):
            if d == 1:
                hv_ref[g] = h.astype(BF16)
                continue
            per_class = tm // d
            for r in range(d):
                for c in range(n_lane_blocks):
                    hv_ref[g, pl.ds(r * per_class, per_class), c * lanes:(c + 1) * lanes] = (
                        hs_ref[c, pl.ds(r, per_class, stride=d), :].astype(BF16))

    tn = w_ref.shape[1]
    o_ref[...] = _dot(hv_ref[j // (cols_per_branch // tn)], w_ref[...]).astype(BF16)


def _qkv_proj(geom, x, mod4, layer, norm_g, w_bf16, tn):
    t, d = x.shape
    n = w_bf16.shape[1]
    tm = ATTN_SUPER
    n_br = len(DIL_CONFIGS)
    cols_per_branch = n // n_br
    assert cols_per_branch % tn == 0
    return pl.pallas_call(
        functools.partial(_qkv_kernel, cols_per_branch=cols_per_branch),
        out_shape=jax.ShapeDtypeStruct((t, n), BF16),
        grid=(t // tm, n // tn),
        in_specs=[
            pl.BlockSpec((tm, d), lambda i, j: (i, 0)),
            pl.BlockSpec((None, None, 6, d), lambda i, j: (layer, geom.seq_of_row(i * tm), 0, 0)),
            pl.BlockSpec((1, d), lambda i, j: (0, 0)),
            pl.BlockSpec((d, tn), lambda i, j: (0, j)),
        ],
        out_specs=pl.BlockSpec((tm, tn), lambda i, j: (i, j)),
        scratch_shapes=[pltpu.VMEM((d // LANES, tm, LANES), F32), pltpu.VMEM((n_br, tm, d), BF16)],
        compiler_params=pltpu.CompilerParams(
            dimension_semantics=("arbitrary", "arbitrary"), vmem_limit_bytes=VMEM_LIMIT),
        name="qkv_proj",
    )(x, mod4, norm_g.reshape(1, d), w_bf16)


def _t5_bucket(rel):
    half_b = NUM_BUCKETS // 2
    max_exact = half_b // 2
    n = jnp.abs(rel)
    large = max_exact + (jnp.log(jnp.maximum(n, 1).astype(F32) / max_exact)
                         / math.log(MAX_DISTANCE / max_exact) * (half_b - max_exact)).astype(I32)
    large = jnp.minimum(large, half_b - 1)
    return jnp.where(rel > 0, half_b, 0) + jnp.where(n < max_exact, n, large)


def _attn_bias_tables(rel_bias_table):
    half = ATTN_HALF
    qi = jnp.arange(half)[:, None]
    ki = jnp.arange(3 * half)[None, :]
    off = ki - half - qi
    in_band = jnp.abs(off) <= half
    per_branch = []
    for g, (window, dilation) in enumerate(DIL_CONFIGS):
        assert window // (2 * dilation) == half
        tab = rel_bias_table[:, g * HEADS_PER_GROUP:(g + 1) * HEADS_PER_GROUP]
        b = jnp.transpose(tab[_t5_bucket(off * dilation)], (2, 0, 1)).astype(F32)
        variants = []
        for var in range(4):
            ok = in_band
            if var & 1:
                ok = ok & (ki >= half)
            if var & 2:
                ok = ok & (ki < 2 * half)
            variants.append(jnp.where(ok[None], b, NEG))
        per_branch.append(jnp.stack(variants, axis=1))
    return jnp.stack(per_branch, axis=1)


def _attn_branch(g, d, q_ref, k_ref, v_ref, bias_ref, o_ref, mx_ref, wt_ref, os_ref, ls_ref,
                 sc_ref, pr_ref, dn_ref, lt_ref, seq_len):
    half = ATTN_HALF
    scale = HEAD_DIM ** -0.5
    units = ATTN_SUPER // half
    nbc = units // d
    n_sc = seq_len // ATTN_SUPER
    last = g == len(DIL_CONFIGS) - 1

    def rows(ref, start):
        return ref[pl.ds(pl.multiple_of(start, half), half), :]

    def super_chunk(sc, carry):
        base_sc = pl.multiple_of(sc * ATTN_SUPER, ATTN_SUPER)

        def neighbours(u):
            n = u % nbc
            at_start = n == 0
            at_end = n == nbc - 1
            prev = jnp.where(at_start, base_sc - ATTN_SUPER + (u + nbc - 1) * half, base_sc + (u - 1) * half)
            nxt = jnp.where(at_end, base_sc + ATTN_SUPER + (u - nbc + 1) * half, base_sc + (u + 1) * half)
            no_prev = jnp.logical_and(at_start, sc == 0)
            no_next = jnp.logical_and(at_end, sc == n_sc - 1)
            prev = jnp.where(no_prev, base_sc + u * half, prev)
            nxt = jnp.where(no_next, base_sc + u * half, nxt)
            return prev, nxt, no_prev.astype(I32) + 2 * no_next.astype(I32)

        def window(ref, u):
            prev, nxt, var = neighbours(u)
            return jnp.concatenate([rows(ref, prev), rows(ref, base_sc + u * half), rows(ref, nxt)], axis=0), var

        def score_unit(u, c2):
            kk, var = window(k_ref, u)
            q = rows(q_ref, base_sc + u * half)
            s = lax.dot_general(q, kk, NT_DIMS, preferred_element_type=F32) * scale + bias_ref[g, var]
            sc_ref[pl.ds(pl.multiple_of(u * half, half), half), :] = s
            return c2

        lax.fori_loop(0, units, score_unit, 0, unroll=ATTN_UNROLL)

        s = sc_ref[...]
        m = jnp.max(s, axis=-1, keepdims=True)
        p = jnp.exp(s - m)
        den = jnp.sum(p, axis=-1, keepdims=True)
        pr_ref[...] = p.astype(BF16)
        dn_ref[...] = jnp.broadcast_to(den, dn_ref.shape)
        lt_ref[...] = jnp.broadcast_to(m + jnp.log(den), lt_ref.shape)

        def value_unit(u, c2):
            vv, _ = window(v_ref, u)
            blk = pl.ds(pl.multiple_of(u * half, half), half)
            o = _dot(pr_ref[blk, :], vv) / dn_ref[blk, :]
            nat = pl.ds((u % nbc) * (half * d) + u // nbc, half, stride=d)
            os_ref[nat, :] = o
            ls_ref[nat, :] = lt_ref[blk, :]
            return c2

        lax.fori_loop(0, units, value_unit, 0, unroll=ATTN_UNROLL)

        span = pl.ds(base_sc, ATTN_SUPER)
        og, lg = os_ref[...], ls_ref[...]
        if g == 0:
            o_ref[span, :] = og
            mx_ref[span, :] = lg
            wt_ref[span, :] = jnp.ones_like(lg)
        else:
            m_old = mx_ref[span, :]
            m_new = jnp.maximum(m_old, lg)
            a = jnp.exp(m_old - m_new)
            b = jnp.exp(lg - m_new)
            acc = o_ref[span, :] * a + og * b
            wt = wt_ref[span, :] * a + b
            if last:
                o_ref[span, :] = acc / wt
            else:
                o_ref[span, :] = acc
                mx_ref[span, :] = m_new
                wt_ref[span, :] = wt
        return carry

    lax.fori_loop(0, seq_len // ATTN_SUPER, super_chunk, 0, unroll=2)


def _attn_kernel(q_ref, k_ref, v_ref, bias_ref, *rest, seq_len):
    n_scratch = 8
    o_ref = rest[-n_scratch - 1]
    for g, (_, d) in enumerate(DIL_CONFIGS):
        @pl.when(pl.program_id(2) == g)
        def _(g=g, d=d):
            _attn_branch(g, d, q_ref, k_ref, v_ref, bias_ref, o_ref, *rest[-n_scratch:], seq_len)


def _attention_group(qkv, bias, prev_out, *, n_seq, seq_len, row_block0, t_total):
    assert seq_len % ATTN_SUPER == 0
    for _, d in DIL_CONFIGS:
        assert ATTN_SUPER % (ATTN_HALF * d) == 0
    h = HEADS_PER_GROUP
    width = h * HEAD_DIM
    n_br = len(DIL_CONFIGS)

    def col_spec(c):
        return pl.BlockSpec((seq_len, HEAD_DIM), lambda b, hh, g: (row_block0 + b, (g * 3 + c) * h + hh))

    in_specs = [col_spec(c) for c in range(3)]
    in_specs.append(pl.BlockSpec((None, n_br, 4, ATTN_HALF, 3 * ATTN_HALF),
                                 lambda b, hh, g: (hh, 0, 0, 0, 0)))
    args = [qkv] * 3 + [bias]
    aliases = {}
    if prev_out is not None:
        in_specs.append(pl.BlockSpec(memory_space=pl.ANY))
        args.append(prev_out)
        aliases = {len(args) - 1: 0}
    return pl.pallas_call(
        functools.partial(_attn_kernel, seq_len=seq_len),
        out_shape=jax.ShapeDtypeStruct((t_total, width), F32),
        grid=(n_seq, h, n_br),
        in_specs=in_specs,
        out_specs=pl.BlockSpec((seq_len, HEAD_DIM), lambda b, hh, g: (row_block0 + b, hh)),
        scratch_shapes=[pltpu.VMEM((seq_len, HEAD_DIM), F32),
                        pltpu.VMEM((seq_len, HEAD_DIM), F32),
                        pltpu.VMEM((ATTN_SUPER, HEAD_DIM), F32),
                        pltpu.VMEM((ATTN_SUPER, HEAD_DIM), F32),
                        pltpu.VMEM((ATTN_SUPER, 3 * ATTN_HALF), F32),
                        pltpu.VMEM((ATTN_SUPER, 3 * ATTN_HALF), BF16),
                        pltpu.VMEM((ATTN_SUPER, HEAD_DIM), F32),
                        pltpu.VMEM((ATTN_SUPER, HEAD_DIM), F32)],
        input_output_aliases=aliases,
        compiler_params=pltpu.CompilerParams(
            dimension_semantics=("arbitrary", "arbitrary", "arbitrary"), vmem_limit_bytes=VMEM_LIMIT),
        name="dilated_attention",
    )(*args)


def _route(h2, wrh_ref, wrl_ref, rb_ref, utri_ref, run_ref, te_ref, gt_ref, rk_ref, cnt_ref):
    n_exp = wrh_ref.shape[0]
    tm = h2.shape[0]
    per_group = n_exp // N_EXPERT_GROUPS
    h_hi, h_lo = _split_bf16(h2)
    wrh = wrh_ref[...]
    logits = (lax.dot_general(wrh, h_hi, NT_DIMS, preferred_element_type=F32)
              + lax.dot_general(wrh, h_lo, NT_DIMS, preferred_element_type=F32)
              + lax.dot_general(wrl_ref[...], h_hi, NT_DIMS, preferred_element_type=F32))
    scores = jax.nn.sigmoid(logits)
    biased = scores + rb_ref[...]

    gi = lax.broadcasted_iota(I32, (per_group, tm), 0).astype(F32)
    group_vals, group_scores = [], []
    for g in range(N_EXPERT_GROUPS):
        v = biased[g * per_group:(g + 1) * per_group, :]
        m1 = jnp.max(v, axis=0, keepdims=True)
        i1 = jnp.min(jnp.where(v == m1, gi, float(per_group)), axis=0, keepdims=True)
        m2 = jnp.max(jnp.where(gi == i1, -jnp.inf, v), axis=0, keepdims=True)
        group_vals.append(v)
        group_scores.append(m1 + m2)
    gs = jnp.concatenate(group_scores, axis=0)
    g_iota = lax.broadcasted_iota(I32, gs.shape, 0)
    rank = jnp.zeros(gs.shape, I32)
    for gp in range(N_EXPERT_GROUPS):
        row = gs[gp:gp + 1, :]
        beats = jnp.where(row > gs, 1, jnp.where(row == gs, jnp.where(g_iota > gp, 1, 0), 0))
        rank = rank + beats
    keep = rank < TOPK_GROUPS
    cur = jnp.concatenate(
        [jnp.where(keep[g:g + 1, :], group_vals[g], NEG) for g in range(N_EXPERT_GROUPS)], axis=0)

    e_iota = lax.broadcasted_iota(I32, (n_exp, tm), 0).astype(F32)
    picks, gates = [], []
    for _ in range(TOP_K):
        mk = jnp.max(cur, axis=0, keepdims=True)
        ik = jnp.min(jnp.where(cur == mk, e_iota, float(n_exp)), axis=0, keepdims=True)
        sel = e_iota == ik
        gates.append(jnp.sum(jnp.where(sel, scores, 0.0), axis=0, keepdims=True))
        cur = jnp.where(sel, -jnp.inf, cur)
        picks.append(ik)
    gt = jnp.concatenate(gates, axis=0)
    gt = gt / jnp.sum(gt, axis=0, keepdims=True) * ROUTED_SCALE
    te_ref[...] = jnp.concatenate(picks, axis=0).astype(I32)
    gt_ref[...] = gt

    hot = jnp.where(cur == -jnp.inf, 1.0, 0.0)
    before = _dot(hot.astype(BF16), utri_ref[...]) + run_ref[...]
    ranks = [jnp.sum(jnp.where(e_iota == picks[k], before, 0.0), axis=0, keepdims=True)
             for k in range(TOP_K)]
    rk_ref[...] = jnp.concatenate(ranks, axis=0).astype(I32)
    run_ref[...] = run_ref[...] + jnp.sum(hot, axis=1, keepdims=True)
    cnt_ref[...] = run_ref[...]


def _post_common(mix, x_ref, mod_ref, n2_ref, wrh_ref, wrl_ref, rb_ref, utri_ref,
                 x1_ref, h2_ref, te_ref, gt_ref, rk_ref, cnt_ref, run_ref):
    @pl.when(pl.program_id(0) == 0)
    def _():
        run_ref[...] = jnp.zeros(run_ref.shape, F32)

    x1 = x_ref[...] + mod_ref[2:3, :] * mix
    x1_ref[...] = x1
    h2 = _norm_mod(x1, n2_ref[...], mod_ref[4:5, :], mod_ref[3:4, :])
    _store_row_major(h2_ref, 0, _pack_bf16_pair(h2))
    _route(h2, wrh_ref, wrl_ref, rb_ref, utri_ref, run_ref, te_ref, gt_ref, rk_ref, cnt_ref)


def _post_attn_kernel(a_ref, wo_ref, *rest):
    mix = _dot(a_ref[...].astype(BF16), wo_ref[...])
    _post_common(mix, *rest)


def _post_pool_kernel(u_ref, up_ref, un_ref, wg_ref, cs_ref, wo_ref, *rest, geom):
    ext_ref = rest[-1]
    rest = rest[:-1]
    tm, d = u_ref.shape
    halo = POOL_HALO
    row0 = pl.program_id(0) * tm
    pos0, slen = geom.pos_and_len(row0)
    u = u_ref[...]
    ext_ref[pl.ds(halo, tm), :] = u
    ext_ref[pl.ds(0, halo), :] = jnp.where(pos0 > 0, up_ref[...], 0.0)
    ext_ref[pl.ds(halo + tm, halo), :] = jnp.where(pos0 + tm < slen, un_ref[...], 0.0)
    pos = pos0 + lax.broadcasted_iota(I32, (tm, 1), 0)
    gc = d // len(POOL_WINDOWS)
    zs = []
    for g, w in enumerate(POOL_WINDOWS):
        hw = w // 2
        assert hw <= halo
        cols = slice(g * gc, (g + 1) * gc)
        acc = ext_ref[pl.ds(halo - hw, tm), cols]
        for o in range(-hw + 1, hw):
            acc = acc + ext_ref[pl.ds(halo + o, tm), cols]
        cnt = (jnp.minimum(pos + hw, slen) - jnp.maximum(pos - hw, 0)).astype(F32)
        p = acc / cnt - u[:, cols]
        zs.append(_dot(p.astype(BF16), wg_ref[g]) * cs_ref[:, cols])
    z = jnp.concatenate(zs, axis=1)
    mix = _dot(z.astype(BF16), wo_ref[...])
    _post_common(mix, *rest)


def _post_mixer(geom, layer, mixer_args, mixer_specs, body, x, mod4, norm2_g, w_router, router_bias,
                extra_scratch=()):
    t, d = x.shape
    n_exp = w_router.shape[1]
    tm = ROW_TILE
    wr_t = w_router.T
    wr_hi = wr_t.astype(BF16)
    wr_lo = (wr_t - wr_hi.astype(F32)).astype(BF16)
    utri = (jnp.arange(tm)[:, None] < jnp.arange(tm)[None, :]).astype(BF16)
    row = lambda i: (i, 0)
    col = lambda i: (0, i)
    const = lambda i: (0, 0)
    in_specs = list(mixer_specs) + [
        pl.BlockSpec((tm, d), row),
        pl.BlockSpec((None, None, 6, d), lambda i: (layer, geom.seq_of_row(i * tm), 0, 0)),
        pl.BlockSpec((1, d), const),
        pl.BlockSpec((n_exp, d), const),
        pl.BlockSpec((n_exp, d), const),
        pl.BlockSpec((n_exp, 1), const),
        pl.BlockSpec((tm, tm), const),
    ]
    out_shape = [
        jax.ShapeDtypeStruct((t, d), F32),
        jax.ShapeDtypeStruct((t * (d // 2 // LANES), LANES), U32),
        jax.ShapeDtypeStruct((TOP_K, t), I32),
        jax.ShapeDtypeStruct((TOP_K, t), F32),
        jax.ShapeDtypeStruct((TOP_K, t), I32),
        jax.ShapeDtypeStruct((n_exp, 1), F32),
    ]
    out_specs = [
        pl.BlockSpec((tm, d), row),
        pl.BlockSpec((tm * (d // 2 // LANES), LANES), row),
        pl.BlockSpec((TOP_K, tm), col),
        pl.BlockSpec((TOP_K, tm), col),
        pl.BlockSpec((TOP_K, tm), col),
        pl.BlockSpec((n_exp, 1), const),
    ]
    return pl.pallas_call(
        body,
        out_shape=out_shape,
        grid=(t // tm,),
        in_specs=in_specs,
        out_specs=out_specs,
        scratch_shapes=[pltpu.VMEM((n_exp, 1), F32)] + list(extra_scratch),
        compiler_params=pltpu.CompilerParams(
            dimension_semantics=("arbitrary",), vmem_limit_bytes=VMEM_LIMIT),
        name="post_mixer",
    )(*mixer_args, x, mod4, norm2_g.reshape(1, d), wr_hi, wr_lo, router_bias.reshape(n_exp, 1), utri)


def _dest_kernel(te_ref, rk_ref, ps_ref, d_ref):
    n_exp = ps_ref.shape[0]
    te = te_ref[...]
    e_iota = lax.broadcasted_iota(I32, (n_exp, te.shape[1]), 0)
    ps = ps_ref[...]
    rows = [jnp.sum(jnp.where(e_iota == te[k:k + 1, :], ps, 0.0), axis=0, keepdims=True)
            for k in range(TOP_K)]
    d_ref[...] = jnp.concatenate(rows, axis=0).astype(I32) + rk_ref[...]


def _dest_slots(te, rk, pstart):
    k, t = te.shape
    n_exp = pstart.shape[0]
    tm = ROW_TILE
    col = lambda i: (0, i)
    return pl.pallas_call(
        _dest_kernel,
        out_shape=jax.ShapeDtypeStruct((k, t), I32),
        grid=(t // tm,),
        in_specs=[pl.BlockSpec((k, tm), col), pl.BlockSpec((k, tm), col),
                  pl.BlockSpec((n_exp, 1), lambda i: (0, 0))],
        out_specs=pl.BlockSpec((k, tm), col),
        name="moe_dest",
    )(te, rk, pstart.astype(F32).reshape(n_exp, 1))


def _zero_kernel(lb_ref, o_ref):
    o_ref[...] = jnp.zeros(o_ref.shape, o_ref.dtype)


def _zero_tail_blocks(last_blk, p_rows, pieces):
    return pl.pallas_call(
        _zero_kernel,
        out_shape=jax.ShapeDtypeStruct((p_rows, pieces, LANES), U32),
        grid_spec=pltpu.PrefetchScalarGridSpec(
            num_scalar_prefetch=1,
            grid=(last_blk.shape[0],),
            in_specs=[],
            out_specs=pl.BlockSpec((EXPERT_BLOCK, pieces, LANES), lambda e, lb: (lb[e], 0, 0)),
        ),
        name="moe_zero_tails",
    )(last_blk)


def _dispatch_kernel(dest_ref, h_ref, xs_in_ref, xs_ref, sem):
    del xs_in_ref
    pieces = xs_ref.shape[1]
    tt = h_ref.shape[0] // pieces

    def issue(tb, carry):
        t0 = pl.multiple_of(tb * SUBLANES, SUBLANES)
        for j in range(SUBLANES):
            for k in range(TOP_K):
                pltpu.async_copy(h_ref.at[pl.ds((t0 + j) * pieces, pieces), :],
                                 xs_ref.at[dest_ref[(t0 + j) * TOP_K + k]], sem,
                                 priority=k % DMA_QUEUES)
        return carry

    lax.fori_loop(0, tt // SUBLANES, issue, 0)
    for k in range(TOP_K):
        pltpu.make_async_copy(h_ref, h_ref, sem).wait()


def _dispatch(dest, h2, xs_zeroed):
    pieces = xs_zeroed.shape[1]
    t = h2.shape[0] // pieces
    tt = MOE_TOKENS
    return pl.pallas_call(
        _dispatch_kernel,
        out_shape=jax.ShapeDtypeStruct(xs_zeroed.shape, xs_zeroed.dtype),
        grid=(t // tt,),
        in_specs=[
            pl.BlockSpec((TOP_K * tt,), lambda i: (i,), memory_space=pltpu.SMEM),
            pl.BlockSpec((tt * pieces, LANES), lambda i: (i, 0)),
            pl.BlockSpec(memory_space=pl.ANY),
        ],
        out_specs=pl.BlockSpec(memory_space=pl.ANY),
        scratch_shapes=[pltpu.SemaphoreType.DMA(())],
        input_output_aliases={2: 0},
        compiler_params=pltpu.CompilerParams(
            dimension_semantics=("arbitrary",), has_side_effects=True),
        name="moe_dispatch",
    )(dest, h2, xs_zeroed)


def _expert_kernel(be_ref, nu_ref, x_ref, wg_ref, wu_ref, wd_ref, o_ref, wgb_ref, wub_ref, wdb_ref):
    i = pl.program_id(0)

    @pl.when(i < nu_ref[0])
    def _():
        @pl.when(jnp.logical_or(i == 0, be_ref[i] != be_ref[jnp.maximum(i - 1, 0)]))
        def _():
            wgb_ref[...] = wg_ref[...].astype(BF16)
            wub_ref[...] = wu_ref[...].astype(BF16)
            wdb_ref[...] = wd_ref[...].astype(BF16)

        half = wgb_ref.shape[0] // 2
        pieces = half // LANES
        for c in range(EXPERT_BLOCK // EXPERT_CHUNK):
            lo, hi = _unpack_bf16_pair(_load_row_major(x_ref, c * EXPERT_CHUNK, EXPERT_CHUNK, pieces))
            lo, hi = lo.astype(BF16), hi.astype(BF16)

            def proj(w_ref, lo=lo, hi=hi):
                return _dot(lo, w_ref[pl.ds(0, half), :]) + _dot(hi, w_ref[pl.ds(half, half), :])

            hid = _silu(proj(wgb_ref)) * proj(wub_ref)
            _store_row_major(o_ref, c * EXPERT_CHUNK, _pack_bf16_pair(_dot(hid.astype(BF16), wdb_ref[...])))


def _experts(layer, blk_expert, n_used, xs, wg, wu, wd):
    d, de = wg.shape[2], wg.shape[3]
    pieces = d // 2 // LANES
    p_rows = xs.shape[0] // pieces
    blk = lambda i, be, nu: (jnp.minimum(i, nu[0] - 1), 0)
    wsel = lambda i, be, nu: (layer, be[i], 0, 0)
    return pl.pallas_call(
        _expert_kernel,
        out_shape=jax.ShapeDtypeStruct(xs.shape, U32),
        grid_spec=pltpu.PrefetchScalarGridSpec(
            num_scalar_prefetch=2,
            grid=(p_rows // EXPERT_BLOCK,),
            in_specs=[
                pl.BlockSpec((EXPERT_BLOCK * pieces, LANES), blk),
                pl.BlockSpec((None, None, d, de), wsel),
                pl.BlockSpec((None, None, d, de), wsel),
                pl.BlockSpec((None, None, de, d), wsel),
            ],
            out_specs=pl.BlockSpec((EXPERT_BLOCK * pieces, LANES), blk),
            scratch_shapes=[pltpu.VMEM((d, de), BF16), pltpu.VMEM((d, de), BF16), pltpu.VMEM((de, d), BF16)],
        ),
        compiler_params=pltpu.CompilerParams(
            dimension_semantics=("arbitrary",), vmem_limit_bytes=VMEM_LIMIT),
        name="moe_experts",
    )(blk_expert, n_used, xs, wg, wu, wd)


def _combine_kernel(dest_ref, dnext_ref, x1_ref, h_ref, gt_ref, mod_ref, wsg_ref, wsu_ref, wsd_ref, fg_ref,
                    ys_ref, *rest, final_norm, first_steps):
    buf_ref, sems = rest[-2:]
    out_refs = rest[:-2]
    tt = x1_ref.shape[0]
    half = x1_ref.shape[1] // 2
    pieces = half // LANES
    i = pl.program_id(0)
    slot = i % 2

    def issue(d_ref, s):
        def body(tb, carry):
            t0 = pl.multiple_of(tb * SUBLANES, SUBLANES)
            for j in range(SUBLANES):
                for k in range(TOP_K):
                    pltpu.async_copy(ys_ref.at[d_ref[(t0 + j) * TOP_K + k]],
                                     buf_ref.at[s, k, pl.ds((t0 + j) * pieces, pieces), :], sems.at[s],
                                     priority=k % DMA_QUEUES)
            return carry

        lax.fori_loop(0, tt // SUBLANES, body, 0)

    @pl.when(i == 0)
    def _():
        issue(dest_ref, 0)

    for s in range(2):
        @pl.when(jnp.logical_and(i + 1 < pl.num_programs(0), slot == 1 - s))
        def _(s=s):
            issue(dnext_ref, s)

    lo, hi = _unpack_bf16_pair(_load_row_major(h_ref, 0, tt, pieces))
    lo, hi = lo.astype(BF16), hi.astype(BF16)

    def proj(w_ref):
        return _dot(lo, w_ref[pl.ds(0, half), :]) + _dot(hi, w_ref[pl.ds(half, half), :])

    hid = _silu(proj(wsg_ref)) * proj(wsu_ref)
    y = _dot(hid.astype(BF16), wsd_ref[...])
    for k in range(TOP_K):
        pltpu.make_async_copy(buf_ref.at[slot, k], buf_ref.at[slot, k], sems.at[slot]).wait()
    gt = gt_ref[...]
    acc_lo = jnp.zeros((tt, half), F32)
    acc_hi = jnp.zeros((tt, half), F32)
    for k in range(TOP_K):
        e_lo, e_hi = _unpack_bf16_pair(_load_row_major(buf_ref.at[slot, k], 0, tt, pieces))
        acc_lo = acc_lo + e_lo * gt[:, k:k + 1]
        acc_hi = acc_hi + e_hi * gt[:, k:k + 1]
    y = y + jnp.concatenate([acc_lo, acc_hi], axis=1)
    x2 = x1_ref[...] + mod_ref[5:6, :] * y
    if not final_norm:
        out_refs[0][...] = x2
        return
    ms = jnp.mean(x2 * x2, axis=-1, keepdims=True)
    x2 = x2 * lax.rsqrt(ms + EPS) * fg_ref[...]

    @pl.when(i < first_steps)
    def _():
        out_refs[0][...] = x2

    @pl.when(i >= first_steps)
    def _():
        out_refs[1][...] = x2


def _combine(geom, layer, dest, x1, h2, gate_t, mod4, wsg, wsu, wsd, final_g, ys, final_norm):
    t, d = x1.shape
    de = wsg.shape[1]
    tt = MOE_TOKENS
    n_steps = t // tt
    first_steps = geom.tp // tt
    row = lambda i: (i, 0)
    const = lambda i: (0, 0)
    if final_norm:
        out_shape = [jax.ShapeDtypeStruct((geom.tp, d), F32), jax.ShapeDtypeStruct((geom.ts, d), F32)]
        out_specs = [pl.BlockSpec((tt, d), lambda i: (jnp.minimum(i, first_steps - 1), 0)),
                     pl.BlockSpec((tt, d), lambda i: (jnp.maximum(i - first_steps, 0), 0))]
    else:
        out_shape = jax.ShapeDtypeStruct((t, d), F32)
        out_specs = pl.BlockSpec((tt, d), row)
    return pl.pallas_call(
        functools.partial(_combine_kernel, final_norm=final_norm, first_steps=first_steps),
        out_shape=out_shape,
        grid=(n_steps,),
        in_specs=[
            pl.BlockSpec((TOP_K * tt,), lambda i: (i,), memory_space=pltpu.SMEM),
            pl.BlockSpec((TOP_K * tt,), lambda i: (jnp.minimum(i + 1, n_steps - 1),), memory_space=pltpu.SMEM),
            pl.BlockSpec((tt, d), row),
            pl.BlockSpec((tt * (d // 2 // LANES), LANES), row),
            pl.BlockSpec((tt, TOP_K), row),
            pl.BlockSpec((None, None, 6, d), lambda i: (layer, geom.seq_of_row(i * tt), 0, 0)),
            pl.BlockSpec((d, de), const),
            pl.BlockSpec((d, de), const),
            pl.BlockSpec((de, d), const),
            pl.BlockSpec((1, d), const),
            pl.BlockSpec(memory_space=pl.ANY),
        ],
        out_specs=out_specs,
        scratch_shapes=[pltpu.VMEM((2, TOP_K, tt * (d // 2 // LANES), LANES), U32), pltpu.SemaphoreType.DMA((2,))],
        compiler_params=pltpu.CompilerParams(
            dimension_semantics=("arbitrary",), vmem_limit_bytes=VMEM_LIMIT),
        name="moe_combine",
    )(dest, dest, x1, h2, gate_t, mod4, wsg, wsu, wsd, final_g.reshape(1, d), ys)


def _moe(geom, layer, x1, h2, te, gt, rk, cnt, mod4, wg, wu, wd, wsg, wsu, wsd, final_g, final_norm):
    t, d = x1.shape
    n_exp = wg.shape[1]
    p_rows = t * TOP_K + n_exp * EXPERT_BLOCK
    n_blk = p_rows // EXPERT_BLOCK
    counts = cnt[:, 0].astype(I32)
    pcounts = (counts + EXPERT_BLOCK - 1) // EXPERT_BLOCK * EXPERT_BLOCK
    pend = jnp.cumsum(pcounts)
    pstart = pend - pcounts
    n_used = pend[-1] // EXPERT_BLOCK
    blk_ids = jnp.arange(n_blk, dtype=I32)
    be = jnp.sum((pend[None, :] <= (blk_ids * EXPERT_BLOCK)[:, None]).astype(I32), axis=1)
    be = jnp.minimum(be, jnp.minimum(be[n_used - 1], n_exp - 1))
    last_blk = jnp.maximum(pend // EXPERT_BLOCK - 1, 0).astype(I32)

    dest = _dest_slots(te, rk, pstart).T.reshape(-1)
    pieces = d // 2 // LANES
    xs = _dispatch(dest, h2, _zero_tail_blocks(last_blk, p_rows, pieces))
    ys = _experts(layer, be, n_used.reshape(1).astype(I32), xs.reshape(p_rows * pieces, LANES), wg, wu, wd)
    ys = ys.reshape(p_rows, pieces, LANES)
    return _combine(geom, layer, dest, x1, h2, gt.T, mod4, wsg, wsu, wsd, final_g, ys, final_norm)


def kernel(x_prompt, x_sample, c_prompt, c_sample, w_mod, b_mod, norm1_g, norm2_g, rel_bias_table, w_in_a, w_out_a, w_in_b, w_grp_b, pool_scale_b, w_out_b, w_router, router_bias, w_gate_e, w_up_e, w_down_e, w_gate_s, w_up_s, w_down_s, final_norm_g):
    bp, sp, d = x_prompt.shape
    bs, ss, _ = x_sample.shape
    geom = _Geom(bp, sp, bs, ss)
    depth = w_mod.shape[0]
    tm = ROW_TILE
    assert sp % ATTN_SUPER == 0 and ss % ATTN_SUPER == 0 and geom.tp % ss == 0
    assert sp % tm == 0 and ss % tm == 0 and tm % MOE_TOKENS == 0

    x = jnp.concatenate([x_prompt.reshape(geom.tp, d), x_sample.reshape(geom.ts, d)], axis=0)
    c = jnp.concatenate([c_prompt, c_sample], axis=0)
    mod4 = _modulation(c, w_mod, b_mod).reshape(depth, geom.nb, 6, d)
    bias = _attn_bias_tables(rel_bias_table)

    for i in range(depth):
        j = i // 2
        if i % 2 == 0:
            qkv = _qkv_proj(geom, x, mod4, i, norm1_g[i], w_in_a[j].astype(BF16), 1536)
            attn = _attention_group(qkv, bias, None, n_seq=bp, seq_len=sp, row_block0=0, t_total=geom.t)
            attn = _attention_group(qkv, bias, attn, n_seq=bs, seq_len=ss, row_block0=geom.tp // ss,
                                    t_total=geom.t)
            mixer_args = (attn, w_out_a[j].astype(BF16))
            mixer_specs = (pl.BlockSpec((tm, d), lambda r: (r, 0)),
                           pl.BlockSpec((d, d), lambda r: (0, 0)))
            body, extra = _post_attn_kernel, ()
        else:
            u = _norm_proj(geom, x, mod4, i, norm1_g[i], w_in_b[j].astype(BF16), 1024)
            hb = tm // POOL_HALO
            n_hb = geom.t // POOL_HALO
            mixer_args = (u, u, u, w_grp_b[j].astype(BF16), pool_scale_b[j].reshape(1, d),
                          w_out_b[j].astype(BF16))
            gc = d // len(POOL_WINDOWS)
            mixer_specs = (pl.BlockSpec((tm, d), lambda r: (r, 0)),
                           pl.BlockSpec((POOL_HALO, d), lambda r: (jnp.maximum(r * hb - 1, 0), 0)),
                           pl.BlockSpec((POOL_HALO, d), lambda r: (jnp.minimum((r + 1) * hb, n_hb - 1), 0)),
                           pl.BlockSpec((len(POOL_WINDOWS), gc, gc), lambda r: (0, 0, 0)),
                           pl.BlockSpec((1, d), lambda r: (0, 0)),
                           pl.BlockSpec((d, d), lambda r: (0, 0)))
            body = functools.partial(_post_pool_kernel, geom=geom)
            extra = (pltpu.VMEM((tm + 2 * POOL_HALO, d), F32),)
        x1, h2, te, gt, rk, cnt = _post_mixer(geom, i, mixer_args, mixer_specs, body, x, mod4, norm2_g[i],
                                              w_router[i], router_bias[i], extra)
        x = _moe(geom, i, x1, h2, te, gt, rk, cnt, mod4,
                 w_gate_e, w_up_e, w_down_e,
                 w_gate_s[i].astype(BF16), w_up_s[i].astype(BF16), w_down_s[i].astype(BF16),
                 final_norm_g, final_norm=(i == depth - 1))

    y_prompt, y_sample = x
    return (y_prompt.reshape(bp, sp, d), y_sample.reshape(bs, ss, d))
```
